```python
import math
import jax, jax.numpy as jnp
from jax import lax
import numpy as np


D_MODEL = 2048
BATCH = 4
SEQ = 4096
DEPTH = 2

MEM_LEN = 256
RWKV_WIDTH = D_MODEL // 2
RWKV_HEAD = 64
RWKV_HEADS = RWKV_WIDTH // RWKV_HEAD
DECAY_LORA = 64
AAA_LORA = 64
GATE_LORA = 160
RWKV_COLS = 3 * RWKV_WIDTH + DECAY_LORA + AAA_LORA + GATE_LORA
DIFF_WIDTH = D_MODEL - RWKV_WIDTH
DIFF_VHEAD = 128
DIFF_HEADS = DIFF_WIDTH // DIFF_VHEAD
DIFF_QK = DIFF_VHEAD // 2
ROT_DIMS = DIFF_QK // 4
ROPE_THETA = 500000.0
N_IN = RWKV_COLS + 3 * DIFF_WIDTH
BLOCK_Q = 128
XATTN_HEADS = 4
XATTN_HEAD = 128
XATTN_WIDTH = XATTN_HEADS * XATTN_HEAD
PEER_HEADS = 8
PEER_KEYS = 128
PEER_EXPERTS = PEER_KEYS * PEER_KEYS
PEER_HALF = 128
PEER_TOPK = 16
PEER_CHUNK = 64
LN_EPS = 1e-5
GN_EPS = 64e-5
DEEPNORM_ALPHA = (2.0 * DEPTH) ** 0.25
DEEPNORM_BETA = (8.0 * DEPTH) ** -0.25

kernel_name = 'hybrid_rwkv7_diffattn_peer_block'

F32 = jnp.float32


def _f(t):
    return t.astype(F32)


def layer_norm(x, w, b):
    xf = _f(x)
    mu = xf.mean(-1, keepdims=True)
    var = jnp.mean(jnp.square(xf - mu), -1, keepdims=True)
    return ((xf - mu) * lax.rsqrt(var + LN_EPS) * _f(w) + _f(b)).astype(x.dtype)


def rwkv7_scan(r, w, k, v, a, b):
    def step(state, inp):
        r_t, w_t, k_t, v_t, a_t, b_t = inp
        sa = jnp.einsum('bhij,bhj->bhi', state, a_t)
        state = (state * w_t[:, :, None, :] + sa[..., None] * b_t[:, :, None, :]
                 + v_t[..., None] * k_t[:, :, None, :])
        y = jnp.einsum('bhij,bhj->bhi', state, r_t)
        return state, y
    B, S, H, N = r.shape
    xs = tuple(jnp.moveaxis(t, 1, 0) for t in (r, w, k, v, a, b))
    s0 = jnp.zeros((B, H, N, N), F32)
    _, ys = lax.scan(step, s0, xs)
    return jnp.moveaxis(ys, 0, 1)


def rwkv7_time_mix(p, shift_mu, w0, w_up, a0, a_up, g_up, k_k, k_a, r_k, lnx_w, lnx_b):
    B, S, _ = p.shape
    H, N, C = RWKV_HEADS, RWKV_HEAD, RWKV_WIDTH
    pf = _f(p)
    p_prev = jnp.pad(pf, ((0, 0), (1, 0), (0, 0)))[:, :-1]
    pf = pf + (p_prev - pf) * _f(shift_mu)
    r = pf[..., :C]
    k = pf[..., C:2 * C]
    v = pf[..., 2 * C:3 * C]
    o = 3 * C
    wd = pf[..., o:o + DECAY_LORA]
    o += DECAY_LORA
    ad = pf[..., o:o + AAA_LORA]
    o += AAA_LORA
    gd = pf[..., o:o + GATE_LORA]
    w_log = -jax.nn.softplus(-(_f(w0) + jnp.tanh(wd) @ _f(w_up))) - 0.5
    decay = jnp.exp(-jnp.exp(w_log))
    iclr = jax.nn.sigmoid(_f(a0) + ad @ _f(a_up))
    g = jax.nn.sigmoid(gd) @ _f(g_up)
    heads = lambda t: t.reshape(B, S, H, N)
    kk = heads(k * _f(k_k))
    kk = kk / jnp.maximum(jnp.linalg.norm(kk, axis=-1, keepdims=True), 1e-12)
    k = k * (1.0 + (iclr - 1.0) * _f(k_a))
    rh, kh, vh, ih = heads(r), heads(k), heads(v), heads(iclr)
    y = rwkv7_scan(rh, heads(decay), kh, vh, -kk, kk * ih)
    mu = y.mean(-1, keepdims=True)
    var = jnp.mean(jnp.square(y - mu), -1, keepdims=True)
    y = ((y - mu) * lax.rsqrt(var + GN_EPS)).reshape(B, S, C) * _f(lnx_w) + _f(lnx_b)
    bonus = jnp.sum(rh * kh * _f(r_k), -1, keepdims=True) * vh
    return (y + bonus.reshape(B, S, C)) * g


def partial_rope(t, cos, sin):
    half = ROT_DIMS // 2
    t1 = t[..., :half]
    t2 = t[..., half:ROT_DIMS]
    return jnp.concatenate([t1 * cos - t2 * sin, t2 * cos + t1 * sin, t[..., ROT_DIMS:]], -1)


def diff_attention(q, k, v, positions, lam_q1, lam_k1, lam_q2, lam_k2, subln_w, layer_idx):
    B, S, _ = q.shape
    H = DIFF_HEADS
    q = q.reshape(B, S, H, 2, DIFF_QK)
    k = k.reshape(B, S, H, 2, DIFF_QK)
    v = v.reshape(B, S, H, DIFF_VHEAD).transpose(0, 2, 1, 3)
    inv_freq = ROPE_THETA ** (-(jnp.arange(0, ROT_DIMS, 2, dtype=F32) / ROT_DIMS))
    ang = _f(positions)[..., None] * inv_freq
    cos = jnp.cos(ang)[:, :, None, None, :].astype(q.dtype)
    sin = jnp.sin(ang)[:, :, None, None, :].astype(q.dtype)
    q = partial_rope(q, cos, sin).transpose(0, 2, 3, 1, 4)
    k = partial_rope(k, cos, sin).transpose(0, 2, 3, 1, 4)
    lam_init = 0.8 - 0.6 * math.exp(-0.3 * layer_idx)
    lam = (jnp.exp(jnp.sum(_f(lam_q1) * _f(lam_k1))) - jnp.exp(jnp.sum(_f(lam_q2) * _f(lam_k2)))
           + lam_init)
    scale = DIFF_QK ** -0.5
    outs = []
    for i in range(S // BLOCK_Q):
        start, end = i * BLOCK_Q, (i + 1) * BLOCK_Q
        qb = q[:, :, :, start:end]
        kb = k[:, :, :, :end]
        vb = v[:, :, :end]
        s = jnp.einsum('bhcqd,bhckd->bhcqk', qb, kb, preferred_element_type=F32) * scale
        qpos = start + jnp.arange(BLOCK_Q)
        kpos = jnp.arange(end)
        s = jnp.where(kpos[None, :] <= qpos[:, None], s, -jnp.inf)
        pr = jax.nn.softmax(s, axis=-1)
        attn = pr[:, :, 0] - lam * pr[:, :, 1]
        outs.append(jnp.einsum('bhqk,bhkd->bhqd', attn.astype(vb.dtype), vb,
                               preferred_element_type=F32))
    o = jnp.concatenate(outs, axis=2)
    o = o * lax.rsqrt(jnp.mean(jnp.square(o), -1, keepdims=True) + LN_EPS) * _f(subln_w)
    o = o * (1.0 - lam_init)
    return o.transpose(0, 2, 1, 3).reshape(B, S, DIFF_WIDTH)


def memory_cross_attention(x, mem, wq, wk, wv, wo):
    B, S, _ = x.shape
    M = mem.shape[1]
    q = (x @ wq).reshape(B, S, XATTN_HEADS, XATTN_HEAD)
    k = (mem @ wk).reshape(B, M, XATTN_HEADS, XATTN_HEAD)
    v = (mem @ wv).reshape(B, M, XATTN_HEADS, XATTN_HEAD)
    s = jnp.einsum('bshd,bmhd->bhsm', q, k, preferred_element_type=F32) * (XATTN_HEAD ** -0.5)
    pr = jax.nn.softmax(s, axis=-1).astype(v.dtype)
    o = jnp.einsum('bhsm,bmhd->bshd', pr, v).reshape(B, S, XATTN_WIDTH)
    return o @ wo


def peer_ffn(x, pq, subkeys, peer_u, peer_v):
    B, S, D = x.shape
    T = B * S
    K = PEER_TOPK
    xt = x.reshape(T, D)
    q = (xt @ pq).reshape(T, PEER_HEADS, 2, PEER_HALF)
    s = jnp.einsum('thcd,ckd->thck', q, subkeys, preferred_element_type=F32)
    sv, si = lax.top_k(s, K)
    cand = (sv[:, :, 0, :, None] + sv[:, :, 1, None, :]).reshape(T, PEER_HEADS, K * K)
    fv, fi = lax.top_k(cand, K)
    i1 = jnp.take_along_axis(si[:, :, 0], fi // K, axis=-1)
    i2 = jnp.take_along_axis(si[:, :, 1], fi % K, axis=-1)
    experts = (i1 * PEER_KEYS + i2).reshape(T, PEER_HEADS * K)
    gates = jax.nn.softmax(fv, axis=-1).reshape(T, PEER_HEADS * K)
    n_chunks = T // PEER_CHUNK

    def chunk(args):
        xc, ec, gc = args
        hc = jnp.einsum('cd,ced->ce', xc, peer_u[ec], preferred_element_type=F32)
        ac = jax.nn.gelu(hc, approximate=False) * gc
        return jnp.einsum('ce,ced->cd', ac.astype(xc.dtype), peer_v[ec])

    y = lax.map(chunk, (xt.reshape(n_chunks, PEER_CHUNK, D),
                        experts.reshape(n_chunks, PEER_CHUNK, -1),
                        gates.reshape(n_chunks, PEER_CHUNK, -1)))
    return y.reshape(B, S, D).astype(x.dtype)


def setup_inputs(seed: int = 0) -> dict:
    key = jax.random.key(seed)
    ks = jax.random.split(key, 40)
    L = DEPTH
    nrm = lambda k, shape, sc: jax.random.normal(k, shape, F32) * sc
    gain = lambda k, shape: 1.0 + 0.05 * jax.random.normal(k, shape, F32)
    positions = (jnp.arange(SEQ, dtype=jnp.int32)[None, :]
                 + jax.random.randint(ks[2], (BATCH, 1), 0, 1024, dtype=jnp.int32))
    return {
        'x': nrm(ks[0], (BATCH, SEQ, D_MODEL), 1.0),
        'mem': nrm(ks[1], (BATCH, MEM_LEN, D_MODEL), 1.0),
        'positions': positions,
        'w_in': nrm(ks[3], (L, D_MODEL, N_IN), D_MODEL ** -0.5),
        'shift_mu': jax.random.uniform(ks[4], (L, RWKV_COLS), F32),
        'w0': jax.random.uniform(ks[5], (L, RWKV_WIDTH), F32, -6.0, 1.0),
        'w_up': nrm(ks[6], (L, DECAY_LORA, RWKV_WIDTH), 0.5 * DECAY_LORA ** -0.5),
        'a0': nrm(ks[7], (L, RWKV_WIDTH), 0.5),
        'a_up': nrm(ks[8], (L, AAA_LORA, RWKV_WIDTH), AAA_LORA ** -0.5),
        'g_up': nrm(ks[9], (L, GATE_LORA, RWKV_WIDTH), GATE_LORA ** -0.5),
        'k_k': 0.85 + 0.05 * jax.random.normal(ks[10], (L, RWKV_WIDTH), F32),
        'k_a': gain(ks[11], (L, RWKV_WIDTH)),
        'r_k': nrm(ks[12], (L, RWKV_HEADS, RWKV_HEAD), 0.1),
        'lnx_w': gain(ks[13], (L, RWKV_WIDTH)),
        'lnx_b': nrm(ks[14], (L, RWKV_WIDTH), 0.01),
        'lam_q1': nrm(ks[15], (L, DIFF_QK), 0.1),
        'lam_k1': nrm(ks[16], (L, DIFF_QK), 0.1),
        'lam_q2': nrm(ks[17], (L, DIFF_QK), 0.1),
        'lam_k2': nrm(ks[18], (L, DIFF_QK), 0.1),
        'subln_w': gain(ks[19], (L, DIFF_VHEAD)),
        'w_out': nrm(ks[20], (L, D_MODEL, D_MODEL), DEEPNORM_BETA * D_MODEL ** -0.5),
        'ln1_w': gain(ks[21], (L, D_MODEL)),
        'ln1_b': nrm(ks[22], (L, D_MODEL), 0.01),
        'xq': nrm(ks[23], (L, D_MODEL, XATTN_WIDTH), D_MODEL ** -0.5),
        'xk': nrm(ks[24], (L, D_MODEL, XATTN_WIDTH), D_MODEL ** -0.5),
        'xv': nrm(ks[25], (L, D_MODEL, XATTN_WIDTH), D_MODEL ** -0.5),
        'xo': nrm(ks[26], (L, XATTN_WIDTH, D_MODEL), DEEPNORM_BETA * XATTN_WIDTH ** -0.5),
        'ln2_w': gain(ks[27], (L, D_MODEL)),
        'ln2_b': nrm(ks[28], (L, D_MODEL), 0.01),
        'pq': nrm(ks[29], (L, D_MODEL, PEER_HEADS * 2 * PEER_HALF), D_MODEL ** -0.5),
        'subkeys': nrm(ks[30], (L, 2, PEER_KEYS, PEER_HALF), PEER_HALF ** -0.5),
        'peer_u': nrm(ks[31], (L, PEER_EXPERTS, D_MODEL), D_MODEL ** -0.5),
        'peer_v': nrm(ks[32], (L, PEER_EXPERTS, D_MODEL), DEEPNORM_BETA * PEER_HEADS ** -0.5),
        'ln3_w': gain(ks[33], (L, D_MODEL)),
        'ln3_b': nrm(ks[34], (L, D_MODEL), 0.01),
    }


def reference(x, mem, positions, w_in, shift_mu, w0, w_up, a0, a_up, g_up, k_k, k_a, r_k,
              lnx_w, lnx_b, lam_q1, lam_k1, lam_q2, lam_k2, subln_w, w_out, ln1_w, ln1_b,
              xq, xk, xv, xo, ln2_w, ln2_b, pq, subkeys, peer_u, peer_v, ln3_w, ln3_b):
    for l in range(DEPTH):
        proj = x @ w_in[l]
        p_rwkv = proj[..., :RWKV_COLS]
        o = RWKV_COLS
        q = proj[..., o:o + DIFF_WIDTH]
        k = proj[..., o + DIFF_WIDTH:o + 2 * DIFF_WIDTH]
        v = proj[..., o + 2 * DIFF_WIDTH:o + 3 * DIFF_WIDTH]
        y_a = rwkv7_time_mix(p_rwkv, shift_mu[l], w0[l], w_up[l], a0[l], a_up[l], g_up[l],
                             k_k[l], k_a[l], r_k[l], lnx_w[l], lnx_b[l])
        y_b = diff_attention(q, k, v, positions, lam_q1[l], lam_k1[l], lam_q2[l], lam_k2[l],
                             subln_w[l], l)
        mix = jnp.concatenate([y_a, y_b], axis=-1).astype(x.dtype) @ w_out[l]
        x = layer_norm(DEEPNORM_ALPHA * x + mix, ln1_w[l], ln1_b[l])
        xa = memory_cross_attention(x, mem, xq[l], xk[l], xv[l], xo[l])
        x = layer_norm(DEEPNORM_ALPHA * x + xa, ln2_w[l], ln2_b[l])
        xf = peer_ffn(x, pq[l], subkeys[l], peer_u[l], peer_v[l])
        x = layer_norm(DEEPNORM_ALPHA * x + xf, ln3_w[l], ln3_b[l])
    return x
```

```python
import functools
import math

import jax
import jax.numpy as jnp
from jax import lax
from jax.experimental import pallas as pl
from jax.experimental.pallas import tpu as pltpu

F32 = jnp.float32
BF16 = jnp.bfloat16

D_MODEL = 2048
DEPTH = 2
RWKV_WIDTH = 1024
RWKV_HEAD = 64
RWKV_HEADS = 16
DECAY_LORA = 64
AAA_LORA = 64
GATE_LORA = 160
LORA_COLS = DECAY_LORA + AAA_LORA + GATE_LORA
LORA_PAD = 384
RWKV_COLS = 3 * RWKV_WIDTH + LORA_COLS
DIFF_WIDTH = 1024
DIFF_VHEAD = 128
DIFF_HEADS = 8
DIFF_QK = 64
ROT_DIMS = 16
ROPE_THETA = 500000.0
XATTN_HEADS = 4
XATTN_HEAD = 128
XATTN_WIDTH = 512
PEER_HEADS = 8
PEER_KEYS = 128
PEER_EXPERTS = PEER_KEYS * PEER_KEYS
PEER_TOPK = 16
PEER_SLOTS = PEER_HEADS * PEER_TOPK
LN_EPS = 1e-5
GN_EPS = 64e-5
DEEPNORM_ALPHA = (2.0 * DEPTH) ** 0.25

LANES = 128
D_CHUNKS = D_MODEL // LANES
SCAN_CHUNK = 64
SCAN_HEADS_PER_STEP = 8
VMEM_LIMIT = 48 * 1024 * 1024

_NT = (((1,), (1,)), ((), ()))
_TN = (((0,), (0,)), ((), ()))


def _cparams(*sem):
    return pltpu.CompilerParams(dimension_semantics=sem, vmem_limit_bytes=VMEM_LIMIT)


def _layer_norm(z, w, b):
    mu = jnp.mean(z, axis=-1, keepdims=True)
    zc = z - mu
    var = jnp.mean(zc * zc, axis=-1, keepdims=True)
    return zc * lax.rsqrt(var + LN_EPS) * w + b


def _mm_kernel(a_ref, w_ref, o_ref):
    o_ref[...] = jnp.dot(a_ref[...].astype(BF16), w_ref[...],
                         preferred_element_type=F32).astype(o_ref.dtype)


def _matmul(a, w, out_dtype, tm, tn):
    M, K = a.shape
    N = w.shape[1]
    tm, tn = min(tm, M), min(tn, N)
    return pl.pallas_call(
        _mm_kernel,
        grid=(N // tn, M // tm),
        in_specs=[pl.BlockSpec((tm, K), lambda j, i: (i, 0)),
                  pl.BlockSpec((K, tn), lambda j, i: (0, j))],
        out_specs=pl.BlockSpec((tm, tn), lambda j, i: (i, j)),
        out_shape=jax.ShapeDtypeStruct((M, N), out_dtype),
        compiler_params=_cparams("parallel", "parallel"),
        name="matmul",
    )(a, w)


def _rwkv_prep_kernel(r_ref, k_ref, v_ref, l_ref, rp_ref, kp_ref, vp_ref, lp_ref,
                      mur_ref, muk_ref, muv_ref, mul_ref, w0_ref, a0_ref, kk_ref, ka_ref,
                      wup_ref, aup_ref, gup_ref,
                      ro_ref, lw_ref, k2_ref, vo_ref, kkr_ref, ic_ref, g_ref, *, tiles_per_seq):
    i = pl.program_id(0)
    first = (i % tiles_per_seq) == 0

    def shifted(cur_ref, prev_ref, mu_ref):
        p = cur_ref[...]
        tm = p.shape[0]
        prev_row = jnp.where(first, 0.0, prev_ref[7:8, :])
        row = lax.broadcasted_iota(jnp.int32, p.shape, 0)
        p_prev = jnp.where(row == 0, prev_row, pltpu.roll(p, 1, 0))
        del tm
        return p + (p_prev - p) * mu_ref[...]

    r = shifted(r_ref, rp_ref, mur_ref)
    k = shifted(k_ref, kp_ref, muk_ref)
    v = shifted(v_ref, vp_ref, muv_ref)
    lo = shifted(l_ref, lp_ref, mul_ref)

    wl = w0_ref[...] + jnp.dot(jnp.tanh(lo).astype(BF16), wup_ref[...], preferred_element_type=F32)
    nz = -wl
    softplus = jnp.maximum(nz, 0.0) + jnp.log(1.0 + jnp.exp(-jnp.abs(nz)))
    w_log = -softplus - 0.5
    lw_ref[...] = -jnp.exp(w_log)
    al = a0_ref[...] + jnp.dot(lo.astype(BF16), aup_ref[...], preferred_element_type=F32)
    iclr = 1.0 / (1.0 + jnp.exp(-al))
    sg = 1.0 / (1.0 + jnp.exp(-lo))
    g_ref[...] = jnp.dot(sg.astype(BF16), gup_ref[...], preferred_element_type=F32)
    ro_ref[...] = r
    vo_ref[...] = v
    kkr_ref[...] = k * kk_ref[...]
    ic_ref[...] = iclr
    k2_ref[...] = k * (1.0 + (iclr - 1.0) * ka_ref[...])


def _rwkv_prep(main, lora, S, mu_r, mu_k, mu_v, mu_l, w0, a0, k_k, k_a, wup, aup, gup):
    T = main.shape[0]
    tm = min(256, S)
    C = RWKV_WIDTH
    prev = lambda c: (lambda i: (jnp.maximum(i * (tm // 8) - 1, 0), c))
    cur = lambda c: (lambda i: (i, c))
    vec = lambda n: pl.BlockSpec((1, n), lambda i: (0, 0))
    mat = pl.BlockSpec((LORA_PAD, C), lambda i: (0, 0))
    out = pl.BlockSpec((tm, C), lambda i: (i, 0))
    return pl.pallas_call(
        functools.partial(_rwkv_prep_kernel, tiles_per_seq=S // tm),
        grid=(T // tm,),
        in_specs=[pl.BlockSpec((tm, C), cur(0)), pl.BlockSpec((tm, C), cur(1)), pl.BlockSpec((tm, C), cur(2)),
                  pl.BlockSpec((tm, LORA_PAD), cur(0)),
                  pl.BlockSpec((8, C), prev(0)), pl.BlockSpec((8, C), prev(1)), pl.BlockSpec((8, C), prev(2)),
                  pl.BlockSpec((8, LORA_PAD), prev(0)),
                  vec(C), vec(C), vec(C), vec(LORA_PAD), vec(C), vec(C), vec(C), vec(C),
                  mat, mat, mat],
        out_specs=[out] * 7,
        out_shape=[jax.ShapeDtypeStruct((T, C), F32)] * 7,
        compiler_params=_cparams("parallel"),
        name="rwkv_prep",
    )(main, main, main, lora, main, main, main, lora,
      mu_r, mu_k, mu_v, mu_l, w0, a0, k_k, k_a, wup, aup, gup)


def _rwkv_scan_kernel(r_ref, lw_ref, k_ref, v_ref, kkr_ref, ic_ref, rk_ref, lnw_ref, lnb_ref,
                      y_ref, state):
    L = SCAN_CHUNK
    N = RWKV_HEAD

    @pl.when(pl.program_id(1) == 0)
    def _():
        state[...] = jnp.zeros_like(state)

    row = lax.broadcasted_iota(jnp.int32, (L, L), 0)
    col = lax.broadcasted_iota(jnp.int32, (L, L), 1)
    tril_incl = (col <= row).astype(F32)
    eye = (col == row).astype(F32)
    row2 = lax.broadcasted_iota(jnp.int32, (L, 2 * L), 0)
    col2 = lax.broadcasted_iota(jnp.int32, (L, 2 * L), 1) % L
    zeros_ln = jnp.zeros((L, N), BF16)
    bdot = lambda a, b: jnp.dot(a.astype(BF16), b.astype(BF16), preferred_element_type=F32)

    for g in range(SCAN_HEADS_PER_STEP):
        r = r_ref[g]
        lw = lw_ref[g]
        k = k_ref[g]
        v = v_ref[g]
        kkr = kkr_ref[g]
        ic = ic_ref[g]
        lp = jnp.dot(tril_incl, lw, precision=lax.Precision.HIGHEST, preferred_element_type=F32)
        ep = jnp.exp(lp)
        en = jnp.exp(-lp)
        nrm = jnp.sqrt(jnp.sum(kkr * kkr, axis=-1, keepdims=True))
        kk = kkr / jnp.maximum(nrm, 1e-12)
        at = (-kk) * jnp.exp(lp - lw)
        rt = r * ep
        bt = (kk * ic) * en
        kt = k * en
        bk = jnp.concatenate([bt, kt], axis=0).astype(BF16)
        vb = v.astype(BF16)
        m_a = lax.dot_general(at.astype(BF16), bk, _NT, preferred_element_type=F32)
        m_r = lax.dot_general(rt.astype(BF16), bk, _NT, preferred_element_type=F32)
        m_a = jnp.where(col2 < row2, m_a, 0.0)
        m_r = jnp.where(col2 <= row2, m_r, 0.0)
        mp = m_a[:, :L]
        tinv = eye + mp
        for _ in range(5):
            mp = bdot(mp, mp)
            tinv = tinv + bdot(mp, tinv)
        x = jnp.dot(m_a.astype(BF16), jnp.concatenate([zeros_ln, vb], axis=0), preferred_element_type=F32)
        w_mat = bdot(tinv, at)
        u0 = bdot(tinv, x)
        w_b = w_mat.astype(BF16)
        u0_b = u0.astype(BF16)
        rp = rt + jnp.dot(m_r.astype(BF16), jnp.concatenate([w_b, zeros_ln], axis=0), preferred_element_type=F32)
        y0 = jnp.dot(m_r.astype(BF16), jnp.concatenate([u0_b, vb], axis=0), preferred_element_type=F32)
        p_last = ep[L - 1:L, :]
        g_mat = (eye + lax.dot_general(w_b, bt.astype(BF16), _TN, preferred_element_type=F32)) * p_last
        h_mat = lax.dot_general(jnp.concatenate([u0_b, vb], axis=0), bk, _TN,
                                preferred_element_type=F32) * p_last
        st = state[g]
        y = lax.dot_general(rp.astype(BF16), st.astype(BF16), _NT, preferred_element_type=F32) + y0
        state[g] = bdot(st, g_mat) + h_mat
        mu = jnp.mean(y, axis=-1, keepdims=True)
        yc = y - mu
        var = jnp.mean(yc * yc, axis=-1, keepdims=True)
        yn = yc * lax.rsqrt(var + GN_EPS) * lnw_ref[g] + lnb_ref[g]
        bonus = jnp.sum(r * k * rk_ref[g], axis=-1, keepdims=True) * v
        y_ref[g] = yn + bonus


def _rwkv_scan(r, lw, k2, v, kkr, ic, rk, lnw, lnb):
    BH, S, N = r.shape
    G = SCAN_HEADS_PER_STEP
    L = SCAN_CHUNK
    groups_per_batch = RWKV_HEADS // G
    blk = pl.BlockSpec((G, L, N), lambda i, c: (i, c, 0))
    par = pl.BlockSpec((G, 1, N), lambda i, c: (i % groups_per_batch, 0, 0))
    return pl.pallas_call(
        _rwkv_scan_kernel,
        grid=(BH // G, S // L),
        in_specs=[blk] * 6 + [par] * 3,
        out_specs=blk,
        out_shape=jax.ShapeDtypeStruct((BH, S, N), F32),
        scratch_shapes=[pltpu.VMEM((G, N, N), F32)],
        compiler_params=_cparams("parallel", "arbitrary"),
        name="rwkv_scan",
    )(r, lw, k2, v, kkr, ic, rk, lnw, lnb)


def _rope_kernel(q_ref, k_ref, v_ref, pos_ref, qo_ref, ko_ref, vo_ref):
    tm = q_ref.shape[0]
    lane = lax.broadcasted_iota(jnp.int32, (1, LANES), 1)
    d = lane % DIFF_QK
    half = ROT_DIMS // 2
    freq = jnp.exp((d % half).astype(F32) * (-2.0 * math.log(ROPE_THETA) / ROT_DIMS))
    freq = jnp.where(d < ROT_DIMS, freq, 0.0)
    ang = pos_ref[...].astype(F32) * freq
    cos = jnp.cos(ang)
    sin = jnp.sin(ang)
    c_mul = jnp.where(d < ROT_DIMS, cos, 1.0)
    s_lo = jnp.where(d < half, -sin, 0.0)
    s_hi = jnp.where((d >= half) & (d < ROT_DIMS), sin, 0.0)
    del tm

    def rot(t):
        return t * c_mul + pltpu.roll(t, LANES - half, 1) * s_lo + pltpu.roll(t, half, 1) * s_hi

    scale = DIFF_QK ** -0.5
    for h in range(DIFF_HEADS):
        sl = slice(h * LANES, (h + 1) * LANES)
        qo_ref[:, sl] = (rot(q_ref[:, sl]) * scale).astype(BF16)
        ko_ref[:, sl] = rot(k_ref[:, sl]).astype(BF16)
    vo_ref[...] = v_ref[...].astype(BF16)


def _rope(main, pos):
    T = main.shape[0]
    tm = min(256, T)
    C = DIFF_WIDTH
    base = 3 * RWKV_WIDTH // C
    blk = lambda c: pl.BlockSpec((tm, C), lambda i: (i, base + c))
    out = pl.BlockSpec((tm, C), lambda i: (i, 0))
    return pl.pallas_call(
        _rope_kernel,
        grid=(T // tm,),
        in_specs=[blk(0), blk(1), blk(2), pl.BlockSpec((tm, 1), lambda i: (i, 0))],
        out_specs=[out] * 3,
        out_shape=[jax.ShapeDtypeStruct((T, C), BF16)] * 3,
        compiler_params=_cparams("parallel"),
        name="rope",
    )(main, main, main, pos)


def _flash_kernel(q_ref, k_ref, v_ref, lq1_ref, lk1_ref, lq2_ref, lk2_ref, sub_ref, o_ref,
                  qc, m_s, l_s, acc, *, lam_init, tq, tk):
    i = pl.program_id(2)
    j = pl.program_id(3)

    @pl.when(j == 0)
    def _():
        q = q_ref[...]
        lane = lax.broadcasted_iota(jnp.int32, q.shape, 1)
        zero = jnp.zeros_like(q)
        qc[0] = jnp.where(lane < DIFF_QK, q, zero)
        qc[1] = jnp.where(lane >= DIFF_QK, q, zero)
        m_s[...] = jnp.full(m_s.shape, -jnp.inf, F32)
        l_s[...] = jnp.zeros(l_s.shape, F32)
        acc[...] = jnp.zeros(acc.shape, F32)

    @pl.when(j <= i)
    def _():
        k = k_ref[...]
        v = v_ref[...]
        row = lax.broadcasted_iota(jnp.int32, (tq, tk), 0) + i * tq
        col = lax.broadcasted_iota(jnp.int32, (tq, tk), 1) + j * tk
        keep = col <= row
        for c in range(2):
            s = lax.dot_general(qc[c], k, _NT, preferred_element_type=F32)
            s = jnp.where(keep, s, -jnp.inf)
            m_prev = m_s[c]
            m_new = jnp.maximum(m_prev, jnp.max(s, axis=1, keepdims=True))
            a = jnp.exp(m_prev - m_new)
            p = jnp.exp(s - m_new)
            l_s[c] = a * l_s[c] + jnp.sum(p, axis=1, keepdims=True)
            acc[c] = a * acc[c] + jnp.dot(p.astype(BF16), v, preferred_element_type=F32)
            m_s[c] = m_new

    @pl.when(j == i)
    def _():
        lam = (jnp.exp(jnp.sum(lq1_ref[...] * lk1_ref[...], keepdims=True))
               - jnp.exp(jnp.sum(lq2_ref[...] * lk2_ref[...], keepdims=True)) + lam_init)
        o = acc[0] / l_s[0] - lam * (acc[1] / l_s[1])
        o = o * lax.rsqrt(jnp.mean(o * o, axis=-1, keepdims=True) + LN_EPS) * sub_ref[...]
        o_ref[...] = o * (1.0 - lam_init)


def _diff_attention(q, k, v, B, S, lq1, lk1, lq2, lk2, subw, layer_idx):
    T = q.shape[0]
    tq = tk = min(512, S)
    nq = S // tq
    lam_init = 0.8 - 0.6 * math.exp(-0.3 * layer_idx)
    vec = pl.BlockSpec((1, DIFF_QK), lambda b, h, i, j: (0, 0))
    return pl.pallas_call(
        functools.partial(_flash_kernel, lam_init=lam_init, tq=tq, tk=tk),
        grid=(B, DIFF_HEADS, nq, nq),
        in_specs=[pl.BlockSpec((tq, LANES), lambda b, h, i, j: (b * nq + i, h)),
                  pl.BlockSpec((tk, LANES), lambda b, h, i, j: (b * nq + jnp.minimum(j, i), h)),
                  pl.BlockSpec((tk, LANES), lambda b, h, i, j: (b * nq + jnp.minimum(j, i), h)),
                  vec, vec, vec, vec,
                  pl.BlockSpec((1, DIFF_VHEAD), lambda b, h, i, j: (0, 0))],
        out_specs=pl.BlockSpec((tq, LANES), lambda b, h, i, j: (b * nq + i, h)),
        out_shape=jax.ShapeDtypeStruct((T, DIFF_WIDTH), F32),
        scratch_shapes=[pltpu.VMEM((2, tq, LANES), BF16),
                        pltpu.VMEM((2, tq, 1), F32),
                        pltpu.VMEM((2, tq, 1), F32),
                        pltpu.VMEM((2, tq, LANES), F32)],
        compiler_params=_cparams("parallel", "parallel", "parallel", "arbitrary"),
        name="diff_attention",
    )(q, k, v, lq1, lk1, lq2, lk2, subw)


def _outproj_kernel(x_ref, ya_ref, g_ref, yb_ref, wa_ref, wb_ref, lnw_ref, lnb_ref, o_ref):
    ya = (ya_ref[...] * g_ref[...]).astype(BF16)
    mix = (jnp.dot(ya, wa_ref[...], preferred_element_type=F32)
           + jnp.dot(yb_ref[...].astype(BF16), wb_ref[...], preferred_element_type=F32))
    o_ref[...] = _layer_norm(DEEPNORM_ALPHA * x_ref[...] + mix, lnw_ref[...], lnb_ref[...])


def _outproj(x, ya, g, yb, wa, wb, lnw, lnb):
    T, D = x.shape
    tm = min(256, T)
    half = pl.BlockSpec((tm, RWKV_WIDTH), lambda i: (i, 0))
    full = pl.BlockSpec((tm, D), lambda i: (i, 0))
    wsp = pl.BlockSpec((RWKV_WIDTH, D), lambda i: (0, 0))
    vec = pl.BlockSpec((1, D), lambda i: (0, 0))
    return pl.pallas_call(
        _outproj_kernel,
        grid=(T // tm,),
        in_specs=[full, half, half, half, wsp, wsp, vec, vec],
        out_specs=full,
        out_shape=jax.ShapeDtypeStruct((T, D), F32),
        compiler_params=_cparams("parallel"),
        name="outproj_ln",
    )(x, ya, g, yb, wa, wb, lnw, lnb)


def _xattn_kernel(x_ref, wq_ref, k_ref, v_ref, wo_ref, lnw_ref, lnb_ref, o_ref):
    x = x_ref[...]
    q = jnp.dot(x.astype(BF16), wq_ref[...], preferred_element_type=F32) * (XATTN_HEAD ** -0.5)
    outs = []
    for h in range(XATTN_HEADS):
        sl = slice(h * XATTN_HEAD, (h + 1) * XATTN_HEAD)
        s = lax.dot_general(q[:, sl].astype(BF16), k_ref[:, sl], _NT, preferred_element_type=F32)
        p = jnp.exp(s - jnp.max(s, axis=-1, keepdims=True))
        p = p / jnp.sum(p, axis=-1, keepdims=True)
        outs.append(jnp.dot(p.astype(BF16), v_ref[:, sl], preferred_element_type=F32))
    o = jnp.concatenate(outs, axis=1).astype(BF16)
    xa = jnp.dot(o, wo_ref[...], preferred_element_type=F32)
    o_ref[...] = _layer_norm(DEEPNORM_ALPHA * x + xa, lnw_ref[...], lnb_ref[...])


def _xattn(x, S, wq, kmem, vmem, wo, lnw, lnb, mem_len):
    T, D = x.shape
    tm = min(256, S)
    tiles_per_seq = S // tm
    full = pl.BlockSpec((tm, D), lambda i: (i, 0))
    kv = pl.BlockSpec((mem_len, XATTN_WIDTH), lambda i: (i // tiles_per_seq, 0))
    vec = pl.BlockSpec((1, D), lambda i: (0, 0))
    return pl.pallas_call(
        _xattn_kernel,
        grid=(T // tm,),
        in_specs=[full, pl.BlockSpec((D, XATTN_WIDTH), lambda i: (0, 0)), kv, kv,
                  pl.BlockSpec((XATTN_WIDTH, D), lambda i: (0, 0)), vec, vec],
        out_specs=full,
        out_shape=jax.ShapeDtypeStruct((T, D), F32),
        compiler_params=_cparams("parallel"),
        name="xattn_ln",
    )(x, wq, kmem, vmem, wo, lnw, lnb)


def _route_kernel(x_ref, pq_ref, sk_ref, e_ref, g_ref, q_s, sv_s, si_s, cand_s, eid_s):
    tm = x_ref.shape[0]
    K = PEER_TOPK
    NK = PEER_KEYS
    q = jnp.dot(x_ref[...].astype(BF16), pq_ref[...], preferred_element_type=F32)
    for hc in range(2 * PEER_HEADS):
        q_s[hc] = q[:, hc * LANES:(hc + 1) * LANES].astype(BF16)
    neg = -jnp.inf

    for sub in range(tm // LANES):
        tok = slice(sub * LANES, (sub + 1) * LANES)
        rowk = lax.broadcasted_iota(jnp.int32, (NK, LANES), 0).astype(F32)
        rowc = lax.broadcasted_iota(jnp.int32, (K * K, LANES), 0).astype(F32)

        def head(h, carry):
            for c in range(2):
                qhc = q_s[2 * h + c, tok, :]
                s = lax.dot_general(sk_ref[c], qhc, _NT, preferred_element_type=F32)
                for j in range(K):
                    m = jnp.max(s, axis=0, keepdims=True)
                    idx = jnp.min(jnp.where(s == m, rowk, float(NK)), axis=0, keepdims=True)
                    s = jnp.where(rowk == idx, neg, s)
                    sv_s[c, j:j + 1, :] = m
                    si_s[c, j:j + 1, :] = idx
            sv1 = sv_s[1]
            si1 = si_s[1]
            for a in range(K):
                cand_s[a * K:(a + 1) * K, :] = sv_s[0, a:a + 1, :] + sv1
                eid_s[a * K:(a + 1) * K, :] = si_s[0, a:a + 1, :] * float(NK) + si1
            cand = cand_s[...]
            eid = eid_s[...]
            fv, fe = [], []
            for j in range(K):
                m = jnp.max(cand, axis=0, keepdims=True)
                idx = jnp.min(jnp.where(cand == m, rowc, float(K * K)), axis=0, keepdims=True)
                sel = rowc == idx
                fe.append(jnp.max(jnp.where(sel, eid, -1.0), axis=0, keepdims=True))
                cand = jnp.where(sel, neg, cand)
                fv.append(m)
            w = [jnp.exp(fv[j] - fv[0]) for j in range(K)]
            den = w[0]
            for j in range(1, K):
                den = den + w[j]
            for j in range(K):
                sv_s[0, j:j + 1, :] = w[j] / den
                si_s[0, j:j + 1, :] = fe[j]
            base = pl.multiple_of(h * K, K)
            g_ref[pl.ds(base, K), tok] = sv_s[0]
            e_ref[pl.ds(base, K), tok] = si_s[0].astype(jnp.int32)
            return carry

        lax.fori_loop(0, PEER_HEADS, head, 0)


def _route(x, pq, sk):
    T, D = x.shape
    tm = min(256, T)
    out = pl.BlockSpec((PEER_SLOTS, tm), lambda i: (0, i))
    return pl.pallas_call(
        _route_kernel,
        grid=(T // tm,),
        in_specs=[pl.BlockSpec((tm, D), lambda i: (i, 0)),
                  pl.BlockSpec((D, D), lambda i: (0, 0)),
                  pl.BlockSpec((2, PEER_KEYS, LANES), lambda i: (0, 0, 0))],
        out_specs=[out, out],
        out_shape=[jax.ShapeDtypeStruct((PEER_SLOTS, T), jnp.int32),
                   jax.ShapeDtypeStruct((PEER_SLOTS, T), F32)],
        scratch_shapes=[pltpu.VMEM((2 * PEER_HEADS, tm, LANES), BF16),
                        pltpu.VMEM((2, PEER_TOPK, LANES), F32),
                        pltpu.VMEM((2, PEER_TOPK, LANES), F32),
                        pltpu.VMEM((PEER_TOPK * PEER_TOPK, LANES), F32),
                        pltpu.VMEM((PEER_TOPK * PEER_TOPK, LANES), F32)],
        compiler_params=_cparams("parallel"),
        name="peer_route",
    )(x, pq, sk)


EXPERT_TOKENS_PER_STEP = 128
EXPERT_SLOTS = 2


def _expert_kernel(idx_ref, x_ref, gt_ref, lnw_ref, lnb_ref, uv_ref, o_ref, buf, sem):
    TB = x_ref.shape[0]
    NE = PEER_SLOTS

    def gather_copy(row, slot, e):
        return pltpu.make_async_copy(uv_ref.at[row], buf.at[slot, :, e, :], sem.at[slot])

    def issue(t, slot):
        def body(e, c):
            gather_copy(idx_ref[e, t], slot, e).start()
            return c
        lax.fori_loop(0, NE, body, 0)

    def wait(slot):
        def body(e, c):
            gather_copy(0, slot, e).wait()
            return c
        lax.fori_loop(0, NE, body, 0)

    issue(0, 0)
    lane = lax.broadcasted_iota(jnp.int32, (NE, TB), 1)

    def tok(t, c):
        slot = lax.rem(t, EXPERT_SLOTS)

        @pl.when(t + 1 < TB)
        def _():
            issue(t + 1, 1 - slot)

        wait(slot)
        xrow = x_ref[pl.ds(t, 1), :]
        acc = jnp.zeros((NE, LANES), F32)
        for s in range(D_CHUNKS):
            acc = acc + buf[slot, s] * xrow[:, s * LANES:(s + 1) * LANES]
        h = jnp.sum(acc, axis=1, keepdims=True)
        gate = jnp.sum(jnp.where(lane == t, gt_ref[...], 0.0), axis=1, keepdims=True)
        a = 0.5 * h * (1.0 + lax.erf(h * (2.0 ** -0.5))) * gate
        ys = [jnp.sum(buf[slot, D_CHUNKS + s] * a, axis=0, keepdims=True) for s in range(D_CHUNKS)]
        z = DEEPNORM_ALPHA * xrow + jnp.concatenate(ys, axis=1)
        o_ref[t] = _layer_norm(z, lnw_ref[...], lnb_ref[...])
        return c

    lax.fori_loop(0, TB, tok, 0)


def _experts(exp_t, x, gates_t, lnw, lnb, uv):
    T, D = x.shape
    TB = min(EXPERT_TOKENS_PER_STEP, T)
    NE = PEER_SLOTS
    out = pl.pallas_call(
        _expert_kernel,
        grid=(T // TB,),
        in_specs=[pl.BlockSpec((NE, TB), lambda i: (0, i), memory_space=pltpu.SMEM),
                  pl.BlockSpec((TB, D), lambda i: (i, 0)),
                  pl.BlockSpec((NE, TB), lambda i: (0, i)),
                  pl.BlockSpec((1, D), lambda i: (0, 0)),
                  pl.BlockSpec((1, D), lambda i: (0, 0)),
                  pl.BlockSpec(memory_space=pl.ANY)],
        out_specs=pl.BlockSpec((TB, 1, D), lambda i: (i, 0, 0)),
        out_shape=jax.ShapeDtypeStruct((T, 1, D), F32),
        scratch_shapes=[pltpu.VMEM((EXPERT_SLOTS, 2 * D_CHUNKS, NE, LANES), F32),
                        pltpu.SemaphoreType.DMA((EXPERT_SLOTS,))],
        compiler_params=_cparams("arbitrary"),
        name="peer_experts",
    )(exp_t, x, gates_t, lnw, lnb, uv)
    return out.reshape(T, D)


def kernel(x, mem, positions, w_in, shift_mu, w0, w_up, a0, a_up, g_up, k_k, k_a, r_k, lnx_w, lnx_b, lam_q1, lam_k1, lam_q2, lam_k2, subln_w, w_out, ln1_w, ln1_b, xq, xk, xv, xo, ln2_w, ln2_b, pq, subkeys, peer_u, peer_v, ln3_w, ln3_b):
    B, S, D = x.shape
    T = B * S
    M = mem.shape[1]
    H, N, C = RWKV_HEADS, RWKV_HEAD, RWKV_WIDTH
    xt = x.reshape(T, D)
    memt = mem.reshape(B * M, D)
    pos = positions.reshape(T, 1)
    row = lambda a: a.reshape(1, -1)
    to_heads = lambda a: a.reshape(B, S, H, N).transpose(0, 2, 1, 3).reshape(B * H, S, N)
    lora_pad = LORA_PAD - LORA_COLS

    for l in range(DEPTH):
        wl = w_in[l]
        w_main = jnp.concatenate([wl[:, :3 * C], wl[:, RWKV_COLS:]], axis=1).astype(BF16)
        w_lora = jnp.pad(wl[:, 3 * C:RWKV_COLS], ((0, 0), (0, lora_pad))).astype(BF16)
        mu = shift_mu[l]
        mu_l = jnp.pad(mu[3 * C:], (0, lora_pad))
        o1, o2 = DECAY_LORA, DECAY_LORA + AAA_LORA
        wup = jnp.pad(w_up[l], ((0, LORA_PAD - o1), (0, 0))).astype(BF16)
        aup = jnp.pad(a_up[l], ((o1, LORA_PAD - o2), (0, 0))).astype(BF16)
        gup = jnp.pad(g_up[l], ((o2, lora_pad), (0, 0))).astype(BF16)

        main = _matmul(xt, w_main, F32, 512, 1024)
        lora = _matmul(xt, w_lora, F32, 512, LORA_PAD)
        r, lw, k2, v, kkr, ic, g = _rwkv_prep(
            main, lora, S, row(mu[:C]), row(mu[C:2 * C]), row(mu[2 * C:3 * C]), row(mu_l),
            row(w0[l]), row(a0[l]), row(k_k[l]), row(k_a[l]), wup, aup, gup)
        y = _rwkv_scan(to_heads(r), to_heads(lw), to_heads(k2), to_heads(v), to_heads(kkr), to_heads(ic),
                       r_k[l].reshape(H, 1, N), lnx_w[l].reshape(H, 1, N), lnx_b[l].reshape(H, 1, N))
        ya = y.reshape(B, H, S, N).transpose(0, 2, 1, 3).reshape(T, C)

        qr, kr, vb = _rope(main, pos)
        yb = _diff_attention(qr, kr, vb, B, S, row(lam_q1[l]), row(lam_k1[l]), row(lam_q2[l]), row(lam_k2[l]),
                             row(subln_w[l]), l)

        wo = w_out[l].astype(BF16)
        x1 = _outproj(xt, ya, g, yb, wo[:C], wo[C:], row(ln1_w[l]), row(ln1_b[l]))

        kmem = _matmul(memt, xk[l].astype(BF16), BF16, 512, XATTN_WIDTH)
        vmem = _matmul(memt, xv[l].astype(BF16), BF16, 512, XATTN_WIDTH)
        x2 = _xattn(x1, S, xq[l].astype(BF16), kmem, vmem, xo[l].astype(BF16), row(ln2_w[l]), row(ln2_b[l]), M)

        exp_t, gates_t = _route(x2, pq[l].astype(BF16), subkeys[l].astype(BF16))
        uv = jnp.concatenate([peer_u[l].reshape(PEER_EXPERTS, D_CHUNKS, LANES),
                              peer_v[l].reshape(PEER_EXPERTS, D_CHUNKS, LANES)], axis=1)
        xt = _experts(exp_t, x2, gates_t, row(ln3_w[l]), row(ln3_b[l]), uv)
    return xt.reshape(B, S, D)
```

```python
import functools
import math

import jax
import jax.numpy as jnp
from jax import lax
from jax.experimental import pallas as pl
from jax.experimental.pallas import tpu as pltpu

F32 = jnp.float32
BF16 = jnp.bfloat16

D_MODEL = 2048
DEPTH = 2
RWKV_WIDTH = 1024
RWKV_HEAD = 64
RWKV_HEADS = 16
DECAY_LORA = 64
AAA_LORA = 64
GATE_LORA = 160
LORA_COLS = DECAY_LORA + AAA_LORA + GATE_LORA
LORA_PAD = 384
RWKV_COLS = 3 * RWKV_WIDTH + LORA_COLS
DIFF_WIDTH = 1024
DIFF_VHEAD = 128
DIFF_HEADS = 8
DIFF_QK = 64
ROT_DIMS = 16
ROPE_THETA = 500000.0
XATTN_HEADS = 4
XATTN_HEAD = 128
XATTN_WIDTH = 512
PEER_HEADS = 8
PEER_KEYS = 128
PEER_EXPERTS = PEER_KEYS * PEER_KEYS
PEER_TOPK = 16
PEER_SLOTS = PEER_HEADS * PEER_TOPK
LN_EPS = 1e-5
GN_EPS = 64e-5
DEEPNORM_ALPHA = (2.0 * DEPTH) ** 0.25

LANES = 128
D_CHUNKS = D_MODEL // LANES
SCAN_CHUNK = 64
SCAN_HEADS_PER_STEP = 8
VMEM_LIMIT = 48 * 1024 * 1024

_NT = (((1,), (1,)), ((), ()))
_TN = (((0,), (0,)), ((), ()))


def _cparams(*sem):
    return pltpu.CompilerParams(dimension_semantics=sem, vmem_limit_bytes=VMEM_LIMIT)


def _layer_norm(z, w, b):
    mu = jnp.mean(z, axis=-1, keepdims=True)
    zc = z - mu
    var = jnp.mean(zc * zc, axis=-1, keepdims=True)
    return zc * lax.rsqrt(var + LN_EPS) * w + b


def _mm_kernel(a_ref, w_ref, o_ref):
    o_ref[...] = jnp.dot(a_ref[...].astype(BF16), w_ref[...],
                         preferred_element_type=F32).astype(o_ref.dtype)


def _matmul(a, w, out_dtype, tm, tn):
    M, K = a.shape
    N = w.shape[1]
    tm, tn = min(tm, M), min(tn, N)
    return pl.pallas_call(
        _mm_kernel,
        grid=(N // tn, M // tm),
        in_specs=[pl.BlockSpec((tm, K), lambda j, i: (i, 0)),
                  pl.BlockSpec((K, tn), lambda j, i: (0, j))],
        out_specs=pl.BlockSpec((tm, tn), lambda j, i: (i, j)),
        out_shape=jax.ShapeDtypeStruct((M, N), out_dtype),
        compiler_params=_cparams("parallel", "parallel"),
        name="matmul",
    )(a, w)


def _rwkv_prep_kernel(r_ref, k_ref, v_ref, l_ref, rp_ref, kp_ref, vp_ref, lp_ref,
                      mur_ref, muk_ref, muv_ref, mul_ref, w0_ref, a0_ref, kk_ref, ka_ref,
                      wup_ref, aup_ref, gup_ref,
                      ro_ref, lw_ref, k2_ref, vo_ref, kkr_ref, ic_ref, g_ref, *, tiles_per_seq):
    i = pl.program_id(0)
    first = (i % tiles_per_seq) == 0

    def shifted(cur_ref, prev_ref, mu_ref):
        p = cur_ref[...]
        tm = p.shape[0]
        prev_row = jnp.where(first, 0.0, prev_ref[7:8, :])
        row = lax.broadcasted_iota(jnp.int32, p.shape, 0)
        p_prev = jnp.where(row == 0, prev_row, pltpu.roll(p, 1, 0))
        del tm
        return p + (p_prev - p) * mu_ref[...]

    r = shifted(r_ref, rp_ref, mur_ref)
    k = shifted(k_ref, kp_ref, muk_ref)
    v = shifted(v_ref, vp_ref, muv_ref)
    lo = shifted(l_ref, lp_ref, mul_ref)

    wl = w0_ref[...] + jnp.dot(jnp.tanh(lo).astype(BF16), wup_ref[...], preferred_element_type=F32)
    nz = -wl
    softplus = jnp.maximum(nz, 0.0) + jnp.log(1.0 + jnp.exp(-jnp.abs(nz)))
    w_log = -softplus - 0.5
    lw_ref[...] = -jnp.exp(w_log)
    al = a0_ref[...] + jnp.dot(lo.astype(BF16), aup_ref[...], preferred_element_type=F32)
    iclr = 1.0 / (1.0 + jnp.exp(-al))
    sg = 1.0 / (1.0 + jnp.exp(-lo))
    g_ref[...] = jnp.dot(sg.astype(BF16), gup_ref[...], preferred_element_type=F32)
    ro_ref[...] = r
    vo_ref[...] = v
    kkr_ref[...] = k * kk_ref[...]
    ic_ref[...] = iclr
    k2_ref[...] = k * (1.0 + (iclr - 1.0) * ka_ref[...])


def _rwkv_prep(main, lora, S, mu_r, mu_k, mu_v, mu_l, w0, a0, k_k, k_a, wup, aup, gup):
    T = main.shape[0]
    tm = min(256, S)
    C = RWKV_WIDTH
    prev = lambda c: (lambda i: (jnp.maximum(i * (tm // 8) - 1, 0), c))
    cur = lambda c: (lambda i: (i, c))
    vec = lambda n: pl.BlockSpec((1, n), lambda i: (0, 0))
    mat = pl.BlockSpec((LORA_PAD, C), lambda i: (0, 0))
    out = pl.BlockSpec((tm, C), lambda i: (i, 0))
    return pl.pallas_call(
        functools.partial(_rwkv_prep_kernel, tiles_per_seq=S // tm),
        grid=(T // tm,),
        in_specs=[pl.BlockSpec((tm, C), cur(0)), pl.BlockSpec((tm, C), cur(1)), pl.BlockSpec((tm, C), cur(2)),
                  pl.BlockSpec((tm, LORA_PAD), cur(0)),
                  pl.BlockSpec((8, C), prev(0)), pl.BlockSpec((8, C), prev(1)), pl.BlockSpec((8, C), prev(2)),
                  pl.BlockSpec((8, LORA_PAD), prev(0)),
                  vec(C), vec(C), vec(C), vec(LORA_PAD), vec(C), vec(C), vec(C), vec(C),
                  mat, mat, mat],
        out_specs=[out] * 7,
        out_shape=[jax.ShapeDtypeStruct((T, C), F32)] * 7,
        compiler_params=_cparams("parallel"),
        name="rwkv_prep",
    )(main, main, main, lora, main, main, main, lora,
      mu_r, mu_k, mu_v, mu_l, w0, a0, k_k, k_a, wup, aup, gup)


def _rwkv_scan_kernel(r_ref, lw_ref, k_ref, v_ref, kkr_ref, ic_ref, rk_ref, lnw_ref, lnb_ref,
                      y_ref, state):
    L = SCAN_CHUNK
    N = RWKV_HEAD

    @pl.when(pl.program_id(1) == 0)
    def _():
        state[...] = jnp.zeros_like(state)

    row = lax.broadcasted_iota(jnp.int32, (L, L), 0)
    col = lax.broadcasted_iota(jnp.int32, (L, L), 1)
    tril_incl = (col <= row).astype(F32)
    eye = (col == row).astype(F32)
    row2 = lax.broadcasted_iota(jnp.int32, (L, 2 * L), 0)
    col2 = lax.broadcasted_iota(jnp.int32, (L, 2 * L), 1) % L
    zeros_ln = jnp.zeros((L, N), BF16)
    bdot = lambda a, b: jnp.dot(a.astype(BF16), b.astype(BF16), preferred_element_type=F32)

    G = range(SCAN_HEADS_PER_STEP)
    r = [r_ref[g] for g in G]
    lw = [lw_ref[g] for g in G]
    k = [k_ref[g] for g in G]
    v = [v_ref[g] for g in G]
    lp = [jnp.dot(tril_incl, lw[g], precision=lax.Precision.HIGHEST, preferred_element_type=F32) for g in G]
    ep = [jnp.exp(lp[g]) for g in G]
    en = [jnp.exp(-lp[g]) for g in G]
    kk = []
    for g in G:
        kkr = kkr_ref[g]
        nrm = jnp.sqrt(jnp.sum(kkr * kkr, axis=-1, keepdims=True))
        kk.append(kkr / jnp.maximum(nrm, 1e-12))
    at = [((-kk[g]) * jnp.exp(lp[g] - lw[g])).astype(BF16) for g in G]
    rt = [r[g] * ep[g] for g in G]
    bt = [((kk[g] * ic_ref[g]) * en[g]).astype(BF16) for g in G]
    bk = [jnp.concatenate([bt[g], (k[g] * en[g]).astype(BF16)], axis=0) for g in G]
    vb = [v[g].astype(BF16) for g in G]
    m_a = [jnp.where(col2 < row2, lax.dot_general(at[g], bk[g], _NT, preferred_element_type=F32), 0.0) for g in G]
    m_r = [jnp.where(col2 <= row2, lax.dot_general(rt[g].astype(BF16), bk[g], _NT, preferred_element_type=F32),
                     0.0).astype(BF16) for g in G]
    x = [jnp.dot(m_a[g].astype(BF16), jnp.concatenate([zeros_ln, vb[g]], axis=0), preferred_element_type=F32)
         for g in G]
    mp = [m_a[g][:, :L] for g in G]
    tinv = [eye + mp[g] for g in G]
    for _ in range(5):
        mp = [bdot(mp[g], mp[g]) for g in G]
        tinv = [tinv[g] + bdot(mp[g], tinv[g]) for g in G]
    tb = [tinv[g].astype(BF16) for g in G]
    w_b = [jnp.dot(tb[g], at[g], preferred_element_type=F32).astype(BF16) for g in G]
    u0_b = [jnp.dot(tb[g], x[g].astype(BF16), preferred_element_type=F32).astype(BF16) for g in G]
    uv = [jnp.concatenate([u0_b[g], vb[g]], axis=0) for g in G]
    rp = [rt[g] + jnp.dot(m_r[g], jnp.concatenate([w_b[g], zeros_ln], axis=0), preferred_element_type=F32)
          for g in G]
    y0 = [jnp.dot(m_r[g], uv[g], preferred_element_type=F32) for g in G]
    p_last = [ep[g][L - 1:L, :] for g in G]
    g_mat = [(eye + lax.dot_general(w_b[g], bt[g], _TN, preferred_element_type=F32)) * p_last[g] for g in G]
    h_mat = [lax.dot_general(uv[g], bk[g], _TN, preferred_element_type=F32) * p_last[g] for g in G]
    st = [state[g] for g in G]
    y = [lax.dot_general(rp[g].astype(BF16), st[g].astype(BF16), _NT, preferred_element_type=F32) + y0[g]
         for g in G]
    for g in G:
        state[g] = bdot(st[g], g_mat[g]) + h_mat[g]
    for g in G:
        mu = jnp.mean(y[g], axis=-1, keepdims=True)
        yc = y[g] - mu
        var = jnp.mean(yc * yc, axis=-1, keepdims=True)
        yn = yc * lax.rsqrt(var + GN_EPS) * lnw_ref[g] + lnb_ref[g]
        bonus = jnp.sum(r[g] * k[g] * rk_ref[g], axis=-1, keepdims=True) * v[g]
        y_ref[g] = yn + bonus


def _rwkv_scan(r, lw, k2, v, kkr, ic, rk, lnw, lnb):
    BH, S, N = r.shape
    G = SCAN_HEADS_PER_STEP
    L = SCAN_CHUNK
    groups_per_batch = RWKV_HEADS // G
    blk = pl.BlockSpec((G, L, N), lambda i, c: (i, c, 0))
    par = pl.BlockSpec((G, 1, N), lambda i, c: (i % groups_per_batch, 0, 0))
    return pl.pallas_call(
        _rwkv_scan_kernel,
        grid=(BH // G, S // L),
        in_specs=[blk] * 6 + [par] * 3,
        out_specs=blk,
        out_shape=jax.ShapeDtypeStruct((BH, S, N), F32),
        scratch_shapes=[pltpu.VMEM((G, N, N), F32)],
        compiler_params=_cparams("parallel", "arbitrary"),
        name="rwkv_scan",
    )(r, lw, k2, v, kkr, ic, rk, lnw, lnb)


def _rope_kernel(q_ref, k_ref, v_ref, pos_ref, qo_ref, ko_ref, vo_ref):
    tm = q_ref.shape[0]
    lane = lax.broadcasted_iota(jnp.int32, (1, LANES), 1)
    d = lane % DIFF_QK
    half = ROT_DIMS // 2
    freq = jnp.exp((d % half).astype(F32) * (-2.0 * math.log(ROPE_THETA) / ROT_DIMS))
    freq = jnp.where(d < ROT_DIMS, freq, 0.0)
    ang = pos_ref[...].astype(F32) * freq
    cos = jnp.cos(ang)
    sin = jnp.sin(ang)
    c_mul = jnp.where(d < ROT_DIMS, cos, 1.0)
    s_lo = jnp.where(d < half, -sin, 0.0)
    s_hi = jnp.where((d >= half) & (d < ROT_DIMS), sin, 0.0)
    del tm

    def rot(t):
        return t * c_mul + pltpu.roll(t, LANES - half, 1) * s_lo + pltpu.roll(t, half, 1) * s_hi

    scale = DIFF_QK ** -0.5
    for h in range(DIFF_HEADS):
        sl = slice(h * LANES, (h + 1) * LANES)
        qo_ref[:, sl] = (rot(q_ref[:, sl]) * scale).astype(BF16)
        ko_ref[:, sl] = rot(k_ref[:, sl]).astype(BF16)
    vo_ref[...] = v_ref[...].astype(BF16)


def _rope(main, pos):
    T = main.shape[0]
    tm = min(256, T)
    C = DIFF_WIDTH
    base = 3 * RWKV_WIDTH // C
    blk = lambda c: pl.BlockSpec((tm, C), lambda i: (i, base + c))
    out = pl.BlockSpec((tm, C), lambda i: (i, 0))
    return pl.pallas_call(
        _rope_kernel,
        grid=(T // tm,),
        in_specs=[blk(0), blk(1), blk(2), pl.BlockSpec((tm, 1), lambda i: (i, 0))],
        out_specs=[out] * 3,
        out_shape=[jax.ShapeDtypeStruct((T, C), BF16)] * 3,
        compiler_params=_cparams("parallel"),
        name="rope",
    )(main, main, main, pos)


def _flash_kernel(q_ref, k_ref, v_ref, lq1_ref, lk1_ref, lq2_ref, lk2_ref, sub_ref, o_ref,
                  qc, m_s, l_s, acc, *, lam_init, tq, tk):
    i = pl.program_id(2)
    j = pl.program_id(3)

    @pl.when(j == 0)
    def _():
        q = q_ref[...]
        lane = lax.broadcasted_iota(jnp.int32, q.shape, 1)
        zero = jnp.zeros_like(q)
        qc[0] = jnp.where(lane < DIFF_QK, q, zero)
        qc[1] = jnp.where(lane >= DIFF_QK, q, zero)
        m_s[...] = jnp.full(m_s.shape, -jnp.inf, F32)
        l_s[...] = jnp.zeros(l_s.shape, F32)
        acc[...] = jnp.zeros(acc.shape, F32)

    def block(diagonal):
        k = k_ref[...]
        v = v_ref[...]
        if diagonal:
            keep = (lax.broadcasted_iota(jnp.int32, (tq, tk), 1)
                    <= lax.broadcasted_iota(jnp.int32, (tq, tk), 0))
        for c in range(2):
            s = lax.dot_general(qc[c], k, _NT, preferred_element_type=F32)
            if diagonal:
                s = jnp.where(keep, s, -jnp.inf)
            m_prev = m_s[c]
            m_new = jnp.maximum(m_prev, jnp.max(s, axis=1, keepdims=True))
            a = jnp.exp(m_prev - m_new)
            p = jnp.exp(s - m_new)
            l_s[c] = a * l_s[c] + jnp.sum(p, axis=1, keepdims=True)
            acc[c] = a * acc[c] + jnp.dot(p.astype(BF16), v, preferred_element_type=F32)
            m_s[c] = m_new

    @pl.when(j < i)
    def _():
        block(False)

    @pl.when(j == i)
    def _():
        block(True)
        lam = (jnp.exp(jnp.sum(lq1_ref[...] * lk1_ref[...], keepdims=True))
               - jnp.exp(jnp.sum(lq2_ref[...] * lk2_ref[...], keepdims=True)) + lam_init)
        o = acc[0] / l_s[0] - lam * (acc[1] / l_s[1])
        o = o * lax.rsqrt(jnp.mean(o * o, axis=-1, keepdims=True) + LN_EPS) * sub_ref[...]
        o_ref[...] = o * (1.0 - lam_init)


def _diff_attention(q, k, v, B, S, lq1, lk1, lq2, lk2, subw, layer_idx):
    T = q.shape[0]
    tq = tk = min(512, S)
    nq = S // tq
    lam_init = 0.8 - 0.6 * math.exp(-0.3 * layer_idx)
    vec = pl.BlockSpec((1, DIFF_QK), lambda b, h, i, j: (0, 0))
    return pl.pallas_call(
        functools.partial(_flash_kernel, lam_init=lam_init, tq=tq, tk=tk),
        grid=(B, DIFF_HEADS, nq, nq),
        in_specs=[pl.BlockSpec((tq, LANES), lambda b, h, i, j: (b * nq + i, h)),
                  pl.BlockSpec((tk, LANES), lambda b, h, i, j: (b * nq + jnp.minimum(j, i), h)),
                  pl.BlockSpec((tk, LANES), lambda b, h, i, j: (b * nq + jnp.minimum(j, i), h)),
                  vec, vec, vec, vec,
                  pl.BlockSpec((1, DIFF_VHEAD), lambda b, h, i, j: (0, 0))],
        out_specs=pl.BlockSpec((tq, LANES), lambda b, h, i, j: (b * nq + i, h)),
        out_shape=jax.ShapeDtypeStruct((T, DIFF_WIDTH), F32),
        scratch_shapes=[pltpu.VMEM((2, tq, LANES), BF16),
                        pltpu.VMEM((2, tq, 1), F32),
                        pltpu.VMEM((2, tq, 1), F32),
                        pltpu.VMEM((2, tq, LANES), F32)],
        compiler_params=_cparams("parallel", "parallel", "parallel", "arbitrary"),
        name="diff_attention",
    )(q, k, v, lq1, lk1, lq2, lk2, subw)


def _outproj_kernel(x_ref, ya_ref, g_ref, yb_ref, wa_ref, wb_ref, lnw_ref, lnb_ref, o_ref):
    ya = (ya_ref[...] * g_ref[...]).astype(BF16)
    mix = (jnp.dot(ya, wa_ref[...], preferred_element_type=F32)
           + jnp.dot(yb_ref[...].astype(BF16), wb_ref[...], preferred_element_type=F32))
    o_ref[...] = _layer_norm(DEEPNORM_ALPHA * x_ref[...] + mix, lnw_ref[...], lnb_ref[...])


def _outproj(x, ya, g, yb, wa, wb, lnw, lnb):
    T, D = x.shape
    tm = min(256, T)
    half = pl.BlockSpec((tm, RWKV_WIDTH), lambda i: (i, 0))
    full = pl.BlockSpec((tm, D), lambda i: (i, 0))
    wsp = pl.BlockSpec((RWKV_WIDTH, D), lambda i: (0, 0))
    vec = pl.BlockSpec((1, D), lambda i: (0, 0))
    return pl.pallas_call(
        _outproj_kernel,
        grid=(T // tm,),
        in_specs=[full, half, half, half, wsp, wsp, vec, vec],
        out_specs=full,
        out_shape=jax.ShapeDtypeStruct((T, D), F32),
        compiler_params=_cparams("parallel"),
        name="outproj_ln",
    )(x, ya, g, yb, wa, wb, lnw, lnb)


def _xattn_kernel(x_ref, wq_ref, k_ref, v_ref, wo_ref, lnw_ref, lnb_ref, o_ref):
    x = x_ref[...]
    q = jnp.dot(x.astype(BF16), wq_ref[...], preferred_element_type=F32) * (XATTN_HEAD ** -0.5)
    outs = []
    for h in range(XATTN_HEADS):
        sl = slice(h * XATTN_HEAD, (h + 1) * XATTN_HEAD)
        s = lax.dot_general(q[:, sl].astype(BF16), k_ref[:, sl], _NT, preferred_element_type=F32)
        p = jnp.exp(s - jnp.max(s, axis=-1, keepdims=True))
        p = p / jnp.sum(p, axis=-1, keepdims=True)
        outs.append(jnp.dot(p.astype(BF16), v_ref[:, sl], preferred_element_type=F32))
    o = jnp.concatenate(outs, axis=1).astype(BF16)
    xa = jnp.dot(o, wo_ref[...], preferred_element_type=F32)
    o_ref[...] = _layer_norm(DEEPNORM_ALPHA * x + xa, lnw_ref[...], lnb_ref[...])


def _xattn(x, S, wq, kmem, vmem, wo, lnw, lnb, mem_len):
    T, D = x.shape
    tm = min(256, S)
    tiles_per_seq = S // tm
    full = pl.BlockSpec((tm, D), lambda i: (i, 0))
    kv = pl.BlockSpec((mem_len, XATTN_WIDTH), lambda i: (i // tiles_per_seq, 0))
    vec = pl.BlockSpec((1, D), lambda i: (0, 0))
    return pl.pallas_call(
        _xattn_kernel,
        grid=(T // tm,),
        in_specs=[full, pl.BlockSpec((D, XATTN_WIDTH), lambda i: (0, 0)), kv, kv,
                  pl.BlockSpec((XATTN_WIDTH, D), lambda i: (0, 0)), vec, vec],
        out_specs=full,
        out_shape=jax.ShapeDtypeStruct((T, D), F32),
        compiler_params=_cparams("parallel"),
        name="xattn_ln",
    )(x, wq, kmem, vmem, wo, lnw, lnb)


def _route_kernel(x_ref, pq_ref, sk_ref, e_ref, g_ref, q_s, sv_s, si_s, cand_s, eid_s):
    tm = x_ref.shape[0]
    K = PEER_TOPK
    NK = PEER_KEYS
    q = jnp.dot(x_ref[...].astype(BF16), pq_ref[...], preferred_element_type=F32)
    for hc in range(2 * PEER_HEADS):
        q_s[hc] = q[:, hc * LANES:(hc + 1) * LANES].astype(BF16)
    neg = -jnp.inf

    for sub in range(tm // LANES):
        tok = slice(sub * LANES, (sub + 1) * LANES)
        rowk = lax.broadcasted_iota(jnp.int32, (NK, LANES), 0).astype(F32)
        rowc = lax.broadcasted_iota(jnp.int32, (K * K, LANES), 0).astype(F32)

        def head(h, carry):
            for c in range(2):
                qhc = q_s[2 * h + c, tok, :]
                s = lax.dot_general(sk_ref[c], qhc, _NT, preferred_element_type=F32)
                for j in range(K):
                    m = jnp.max(s, axis=0, keepdims=True)
                    idx = jnp.min(jnp.where(s == m, rowk, float(NK)), axis=0, keepdims=True)
                    s = jnp.where(rowk == idx, neg, s)
                    sv_s[c, j:j + 1, :] = m
                    si_s[c, j:j + 1, :] = idx
            sv1 = sv_s[1]
            si1 = si_s[1]
            for a in range(K):
                cand_s[a * K:(a + 1) * K, :] = sv_s[0, a:a + 1, :] + sv1
                eid_s[a * K:(a + 1) * K, :] = si_s[0, a:a + 1, :] * float(NK) + si1
            cand = cand_s[...]
            eid = eid_s[...]
            fv, fe = [], []
            for j in range(K):
                m = jnp.max(cand, axis=0, keepdims=True)
                idx = jnp.min(jnp.where(cand == m, rowc, float(K * K)), axis=0, keepdims=True)
                sel = rowc == idx
                fe.append(jnp.max(jnp.where(sel, eid, -1.0), axis=0, keepdims=True))
                cand = jnp.where(sel, neg, cand)
                fv.append(m)
            w = [jnp.exp(fv[j] - fv[0]) for j in range(K)]
            den = w[0]
            for j in range(1, K):
                den = den + w[j]
            for j in range(K):
                sv_s[0, j:j + 1, :] = w[j] / den
                si_s[0, j:j + 1, :] = fe[j]
            base = pl.multiple_of(h * K, K)
            g_ref[pl.ds(base, K), tok] = sv_s[0]
            e_ref[pl.ds(base, K), tok] = si_s[0].astype(jnp.int32)
            return carry

        lax.fori_loop(0, PEER_HEADS, head, 0)


def _route(x, pq, sk):
    T, D = x.shape
    tm = min(256, T)
    out = pl.BlockSpec((PEER_SLOTS, tm), lambda i: (0, i))
    return pl.pallas_call(
        _route_kernel,
        grid=(T // tm,),
        in_specs=[pl.BlockSpec((tm, D), lambda i: (i, 0)),
                  pl.BlockSpec((D, D), lambda i: (0, 0)),
                  pl.BlockSpec((2, PEER_KEYS, LANES), lambda i: (0, 0, 0))],
        out_specs=[out, out],
        out_shape=[jax.ShapeDtypeStruct((PEER_SLOTS, T), jnp.int32),
                   jax.ShapeDtypeStruct((PEER_SLOTS, T), F32)],
        scratch_shapes=[pltpu.VMEM((2 * PEER_HEADS, tm, LANES), BF16),
                        pltpu.VMEM((2, PEER_TOPK, LANES), F32),
                        pltpu.VMEM((2, PEER_TOPK, LANES), F32),
                        pltpu.VMEM((PEER_TOPK * PEER_TOPK, LANES), F32),
                        pltpu.VMEM((PEER_TOPK * PEER_TOPK, LANES), F32)],
        compiler_params=_cparams("parallel"),
        name="peer_route",
    )(x, pq, sk)


EXPERT_TOKENS_PER_STEP = 128
EXPERT_BUFS = 4
EXPERT_LOOKAHEAD = EXPERT_BUFS - 1


def _expert_kernel(idx_ref, nidx_ref, x_ref, gt_ref, lnw_ref, lnb_ref, uv_ref, o_ref, *scratch):
    bufs, sem = scratch[:EXPERT_BUFS], scratch[EXPERT_BUFS]
    TB = x_ref.shape[0]
    NE = PEER_SLOTS
    step = pl.program_id(0)
    per_chunk = NE // (2 * D_CHUNKS)

    def start(iref, t, e, b):
        pltpu.make_async_copy(uv_ref.at[iref[e, t]], bufs[b].at[:, e, :], sem.at[b]).start()

    def wait(b):
        pltpu.make_async_copy(bufs[b], bufs[b], sem.at[b]).wait()

    @pl.when(step == 0)
    def _():
        for b in range(EXPERT_LOOKAHEAD):
            def body(e, c, b=b):
                start(idx_ref, b, e, b)
                return c
            lax.fori_loop(0, NE, body, 0)

    lane = lax.broadcasted_iota(jnp.int32, (NE, TB), 1)

    def token(t, b, issue):
        buf = bufs[b]
        wait(b)
        xrow = x_ref[pl.ds(t, 1), :]
        acc = jnp.zeros((NE, LANES), F32)
        for s in range(D_CHUNKS):
            for e in range(s * per_chunk, (s + 1) * per_chunk):
                issue(e)
            acc = acc + buf[s] * xrow[:, s * LANES:(s + 1) * LANES]
        h = jnp.sum(acc, axis=1, keepdims=True)
        gate = jnp.sum(jnp.where(lane == t, gt_ref[...], 0.0), axis=1, keepdims=True)
        a = 0.5 * h * (1.0 + lax.erf(h * (2.0 ** -0.5))) * gate
        ys = []
        for s in range(D_CHUNKS):
            for e in range((D_CHUNKS + s) * per_chunk, (D_CHUNKS + s + 1) * per_chunk):
                issue(e)
            ys.append(jnp.sum(buf[D_CHUNKS + s] * a, axis=0, keepdims=True))
        z = DEEPNORM_ALPHA * xrow + jnp.concatenate(ys, axis=1)
        o_ref[t] = _layer_norm(z, lnw_ref[...], lnb_ref[...])

    def ring(t0, last):
        for b in range(EXPERT_BUFS):
            nb = (b + EXPERT_LOOKAHEAD) % EXPERT_BUFS
            ahead = b + EXPERT_LOOKAHEAD
            if last and ahead >= EXPERT_BUFS:
                issue = lambda e, tn=ahead - EXPERT_BUFS, nb=nb: start(nidx_ref, tn, e, nb)
            else:
                issue = lambda e, tn=t0 + ahead, nb=nb: start(idx_ref, tn, e, nb)
            token(t0 + b, b, issue)

    def ring_body(q, c):
        ring(q * EXPERT_BUFS, False)
        return c

    lax.fori_loop(0, TB // EXPERT_BUFS - 1, ring_body, 0)
    ring(TB - EXPERT_BUFS, True)

    @pl.when(step == pl.num_programs(0) - 1)
    def _():
        for b in range(EXPERT_LOOKAHEAD):
            wait(b)


def _experts(exp_t, x, gates_t, lnw, lnb, uv):
    T, D = x.shape
    TB = min(EXPERT_TOKENS_PER_STEP, T)
    NE = PEER_SLOTS
    n = T // TB
    out = pl.pallas_call(
        _expert_kernel,
        grid=(n,),
        in_specs=[pl.BlockSpec((NE, TB), lambda i: (0, i), memory_space=pltpu.SMEM),
                  pl.BlockSpec((NE, TB), lambda i: (0, jnp.minimum(i + 1, n - 1)), memory_space=pltpu.SMEM),
                  pl.BlockSpec((TB, D), lambda i: (i, 0)),
                  pl.BlockSpec((NE, TB), lambda i: (0, i)),
                  pl.BlockSpec((1, D), lambda i: (0, 0)),
                  pl.BlockSpec((1, D), lambda i: (0, 0)),
                  pl.BlockSpec(memory_space=pl.ANY)],
        out_specs=pl.BlockSpec((TB, 1, D), lambda i: (i, 0, 0)),
        out_shape=jax.ShapeDtypeStruct((T, 1, D), F32),
        scratch_shapes=[pltpu.VMEM((2 * D_CHUNKS, NE, LANES), F32)] * EXPERT_BUFS
        + [pltpu.SemaphoreType.DMA((EXPERT_BUFS,))],
        compiler_params=_cparams("arbitrary"),
        name="peer_experts",
    )(exp_t, exp_t, x, gates_t, lnw, lnb, uv)
    return out.reshape(T, D)


def kernel(x, mem, positions, w_in, shift_mu, w0, w_up, a0, a_up, g_up, k_k, k_a, r_k, lnx_w, lnx_b, lam_q1, lam_k1, lam_q2, lam_k2, subln_w, w_out, ln1_w, ln1_b, xq, xk, xv, xo, ln2_w, ln2_b, pq, subkeys, peer_u, peer_v, ln3_w, ln3_b):
    B, S, D = x.shape
    T = B * S
    M = mem.shape[1]
    H, N, C = RWKV_HEADS, RWKV_HEAD, RWKV_WIDTH
    xt = x.reshape(T, D)
    memt = mem.reshape(B * M, D)
    pos = positions.reshape(T, 1)
    row = lambda a: a.reshape(1, -1)
    to_heads = lambda a: a.reshape(B, S, H, N).transpose(0, 2, 1, 3).reshape(B * H, S, N)
    lora_pad = LORA_PAD - LORA_COLS

    for l in range(DEPTH):
        wl = w_in[l]
        w_main = jnp.concatenate([wl[:, :3 * C], wl[:, RWKV_COLS:]], axis=1).astype(BF16)
        w_lora = jnp.pad(wl[:, 3 * C:RWKV_COLS], ((0, 0), (0, lora_pad))).astype(BF16)
        mu = shift_mu[l]
        mu_l = jnp.pad(mu[3 * C:], (0, lora_pad))
        o1, o2 = DECAY_LORA, DECAY_LORA + AAA_LORA
        wup = jnp.pad(w_up[l], ((0, LORA_PAD - o1), (0, 0))).astype(BF16)
        aup = jnp.pad(a_up[l], ((o1, LORA_PAD - o2), (0, 0))).astype(BF16)
        gup = jnp.pad(g_up[l], ((o2, lora_pad), (0, 0))).astype(BF16)

        main = _matmul(xt, w_main, F32, 512, 1024)
        lora = _matmul(xt, w_lora, F32, 512, LORA_PAD)
        r, lw, k2, v, kkr, ic, g = _rwkv_prep(
            main, lora, S, row(mu[:C]), row(mu[C:2 * C]), row(mu[2 * C:3 * C]), row(mu_l),
            row(w0[l]), row(a0[l]), row(k_k[l]), row(k_a[l]), wup, aup, gup)
        y = _rwkv_scan(to_heads(r), to_heads(lw), to_heads(k2), to_heads(v), to_heads(kkr), to_heads(ic),
                       r_k[l].reshape(H, 1, N), lnx_w[l].reshape(H, 1, N), lnx_b[l].reshape(H, 1, N))
        ya = y.reshape(B, H, S, N).transpose(0, 2, 1, 3).reshape(T, C)

        qr, kr, vb = _rope(main, pos)
        yb = _diff_attention(qr, kr, vb, B, S, row(lam_q1[l]), row(lam_k1[l]), row(lam_q2[l]), row(lam_k2[l]),
                             row(subln_w[l]), l)

        wo = w_out[l].astype(BF16)
        x1 = _outproj(xt, ya, g, yb, wo[:C], wo[C:], row(ln1_w[l]), row(ln1_b[l]))

        kmem = _matmul(memt, xk[l].astype(BF16), BF16, 512, XATTN_WIDTH)
        vmem = _matmul(memt, xv[l].astype(BF16), BF16, 512, XATTN_WIDTH)
        x2 = _xattn(x1, S, xq[l].astype(BF16), kmem, vmem, xo[l].astype(BF16), row(ln2_w[l]), row(ln2_b[l]), M)

        exp_t, gates_t = _route(x2, pq[l].astype(BF16), subkeys[l].astype(BF16))
        uv = jnp.concatenate([peer_u[l].reshape(PEER_EXPERTS, D_CHUNKS, LANES),
                              peer_v[l].reshape(PEER_EXPERTS, D_CHUNKS, LANES)], axis=1)
        xt = _experts(exp_t, x2, gates_t, row(ln3_w[l]), row(ln3_b[l]), uv)
    return xt.reshape(B, S, D)
```

```python
import functools
import math

import jax
import jax.numpy as jnp
from jax import lax
from jax.experimental import pallas as pl
from jax.experimental.pallas import tpu as pltpu

F32 = jnp.float32
BF16 = jnp.bfloat16

D_MODEL = 2048
DEPTH = 2
RWKV_WIDTH = 1024
RWKV_HEAD = 64
RWKV_HEADS = 16
DECAY_LORA = 64
AAA_LORA = 64
GATE_LORA = 160
LORA_COLS = DECAY_LORA + AAA_LORA + GATE_LORA
LORA_PAD = 384
RWKV_COLS = 3 * RWKV_WIDTH + LORA_COLS
DIFF_WIDTH = 1024
DIFF_VHEAD = 128
DIFF_HEADS = 8
DIFF_QK = 64
ROT_DIMS = 16
ROPE_THETA = 500000.0
XATTN_HEADS = 4
XATTN_HEAD = 128
XATTN_WIDTH = 512
PEER_HEADS = 8
PEER_KEYS = 128
PEER_EXPERTS = PEER_KEYS * PEER_KEYS
PEER_TOPK = 16
PEER_SLOTS = PEER_HEADS * PEER_TOPK
LN_EPS = 1e-5
GN_EPS = 64e-5
DEEPNORM_ALPHA = (2.0 * DEPTH) ** 0.25

LANES = 128
D_CHUNKS = D_MODEL // LANES
SCAN_CHUNK = 64
SCAN_HEADS_PER_STEP = 8
VMEM_LIMIT = 48 * 1024 * 1024

_NT = (((1,), (1,)), ((), ()))
_TN = (((0,), (0,)), ((), ()))


def _cparams(*sem):
    return pltpu.CompilerParams(dimension_semantics=sem, vmem_limit_bytes=VMEM_LIMIT)


def _layer_norm(z, w, b):
    mu = jnp.mean(z, axis=-1, keepdims=True)
    zc = z - mu
    var = jnp.mean(zc * zc, axis=-1, keepdims=True)
    return zc * lax.rsqrt(var + LN_EPS) * w + b


def _mm_kernel(a_ref, w_ref, o_ref):
    o_ref[...] = jnp.dot(a_ref[...].astype(BF16), w_ref[...],
                         preferred_element_type=F32).astype(o_ref.dtype)


def _matmul(a, w, out_dtype, tm, tn):
    M, K = a.shape
    N = w.shape[1]
    tm, tn = min(tm, M), min(tn, N)
    return pl.pallas_call(
        _mm_kernel,
        grid=(N // tn, M // tm),
        in_specs=[pl.BlockSpec((tm, K), lambda j, i: (i, 0)),
                  pl.BlockSpec((K, tn), lambda j, i: (0, j))],
        out_specs=pl.BlockSpec((tm, tn), lambda j, i: (i, j)),
        out_shape=jax.ShapeDtypeStruct((M, N), out_dtype),
        compiler_params=_cparams("parallel", "parallel"),
        name="matmul",
    )(a, w)


def _rwkv_prep_kernel(r_ref, k_ref, v_ref, l_ref, rp_ref, kp_ref, vp_ref, lp_ref,
                      mur_ref, muk_ref, muv_ref, mul_ref, w0_ref, a0_ref, kk_ref, ka_ref,
                      wup_ref, aup_ref, gup_ref,
                      ro_ref, lw_ref, k2_ref, vo_ref, kkr_ref, ic_ref, g_ref, *, tiles_per_seq):
    i = pl.program_id(0)
    first = (i % tiles_per_seq) == 0

    def shifted(cur_ref, prev_ref, mu_ref):
        p = cur_ref[...]
        tm = p.shape[0]
        prev_row = jnp.where(first, 0.0, prev_ref[7:8, :])
        row = lax.broadcasted_iota(jnp.int32, p.shape, 0)
        p_prev = jnp.where(row == 0, prev_row, pltpu.roll(p, 1, 0))
        del tm
        return p + (p_prev - p) * mu_ref[...]

    r = shifted(r_ref, rp_ref, mur_ref)
    k = shifted(k_ref, kp_ref, muk_ref)
    v = shifted(v_ref, vp_ref, muv_ref)
    lo = shifted(l_ref, lp_ref, mul_ref)

    wl = w0_ref[...] + jnp.dot(jnp.tanh(lo).astype(BF16), wup_ref[...], preferred_element_type=F32)
    nz = -wl
    softplus = jnp.maximum(nz, 0.0) + jnp.log(1.0 + jnp.exp(-jnp.abs(nz)))
    w_log = -softplus - 0.5
    lw_ref[...] = -jnp.exp(w_log)
    al = a0_ref[...] + jnp.dot(lo.astype(BF16), aup_ref[...], preferred_element_type=F32)
    iclr = 1.0 / (1.0 + jnp.exp(-al))
    sg = 1.0 / (1.0 + jnp.exp(-lo))
    g_ref[...] = jnp.dot(sg.astype(BF16), gup_ref[...], preferred_element_type=F32)
    ro_ref[...] = r
    vo_ref[...] = v
    kkr_ref[...] = k * kk_ref[...]
    ic_ref[...] = iclr
    k2_ref[...] = k * (1.0 + (iclr - 1.0) * ka_ref[...])


def _rwkv_prep(main, lora, S, mu_r, mu_k, mu_v, mu_l, w0, a0, k_k, k_a, wup, aup, gup):
    T = main.shape[0]
    tm = min(256, S)
    C = RWKV_WIDTH
    prev = lambda c: (lambda i: (jnp.maximum(i * (tm // 8) - 1, 0), c))
    cur = lambda c: (lambda i: (i, c))
    vec = lambda n: pl.BlockSpec((1, n), lambda i: (0, 0))
    mat = pl.BlockSpec((LORA_PAD, C), lambda i: (0, 0))
    out = pl.BlockSpec((tm, C), lambda i: (i, 0))
    return pl.pallas_call(
        functools.partial(_rwkv_prep_kernel, tiles_per_seq=S // tm),
        grid=(T // tm,),
        in_specs=[pl.BlockSpec((tm, C), cur(0)), pl.BlockSpec((tm, C), cur(1)), pl.BlockSpec((tm, C), cur(2)),
                  pl.BlockSpec((tm, LORA_PAD), cur(0)),
                  pl.BlockSpec((8, C), prev(0)), pl.BlockSpec((8, C), prev(1)), pl.BlockSpec((8, C), prev(2)),
                  pl.BlockSpec((8, LORA_PAD), prev(0)),
                  vec(C), vec(C), vec(C), vec(LORA_PAD), vec(C), vec(C), vec(C), vec(C),
                  mat, mat, mat],
        out_specs=[out] * 7,
        out_shape=[jax.ShapeDtypeStruct((T, C), F32)] * 7,
        compiler_params=_cparams("parallel"),
        name="rwkv_prep",
    )(main, main, main, lora, main, main, main, lora,
      mu_r, mu_k, mu_v, mu_l, w0, a0, k_k, k_a, wup, aup, gup)


def _rwkv_scan_kernel(r_ref, lw_ref, k_ref, v_ref, kkr_ref, ic_ref, rk_ref, lnw_ref, lnb_ref,
                      y_ref, state):
    L = SCAN_CHUNK
    N = RWKV_HEAD

    @pl.when(pl.program_id(1) == 0)
    def _():
        state[...] = jnp.zeros_like(state)

    row = lax.broadcasted_iota(jnp.int32, (L, L), 0)
    col = lax.broadcasted_iota(jnp.int32, (L, L), 1)
    tril_incl = (col <= row).astype(F32)
    eye = (col == row).astype(F32)
    row2 = lax.broadcasted_iota(jnp.int32, (L, 2 * L), 0)
    col2 = lax.broadcasted_iota(jnp.int32, (L, 2 * L), 1) % L
    zeros_ln = jnp.zeros((L, N), BF16)
    bdot = lambda a, b: jnp.dot(a.astype(BF16), b.astype(BF16), preferred_element_type=F32)

    G = range(SCAN_HEADS_PER_STEP)
    r = [r_ref[g] for g in G]
    lw = [lw_ref[g] for g in G]
    k = [k_ref[g] for g in G]
    v = [v_ref[g] for g in G]
    lp = [jnp.dot(tril_incl, lw[g], precision=lax.Precision.HIGHEST, preferred_element_type=F32) for g in G]
    ep = [jnp.exp(lp[g]) for g in G]
    en = [jnp.exp(-lp[g]) for g in G]
    kk = []
    for g in G:
        kkr = kkr_ref[g]
        nrm = jnp.sqrt(jnp.sum(kkr * kkr, axis=-1, keepdims=True))
        kk.append(kkr / jnp.maximum(nrm, 1e-12))
    at = [((-kk[g]) * jnp.exp(lp[g] - lw[g])).astype(BF16) for g in G]
    rt = [r[g] * ep[g] for g in G]
    bt = [((kk[g] * ic_ref[g]) * en[g]).astype(BF16) for g in G]
    bk = [jnp.concatenate([bt[g], (k[g] * en[g]).astype(BF16)], axis=0) for g in G]
    vb = [v[g].astype(BF16) for g in G]
    m_a = [jnp.where(col2 < row2, lax.dot_general(at[g], bk[g], _NT, preferred_element_type=F32), 0.0) for g in G]
    m_r = [jnp.where(col2 <= row2, lax.dot_general(rt[g].astype(BF16), bk[g], _NT, preferred_element_type=F32),
                     0.0).astype(BF16) for g in G]
    x = [jnp.dot(m_a[g].astype(BF16), jnp.concatenate([zeros_ln, vb[g]], axis=0), preferred_element_type=F32)
         for g in G]
    mp = [m_a[g][:, :L] for g in G]
    tinv = [eye + mp[g] for g in G]
    for _ in range(5):
        mp = [bdot(mp[g], mp[g]) for g in G]
        tinv = [tinv[g] + bdot(mp[g], tinv[g]) for g in G]
    tb = [tinv[g].astype(BF16) for g in G]
    w_b = [jnp.dot(tb[g], at[g], preferred_element_type=F32).astype(BF16) for g in G]
    u0_b = [jnp.dot(tb[g], x[g].astype(BF16), preferred_element_type=F32).astype(BF16) for g in G]
    uv = [jnp.concatenate([u0_b[g], vb[g]], axis=0) for g in G]
    rp = [rt[g] + jnp.dot(m_r[g], jnp.concatenate([w_b[g], zeros_ln], axis=0), preferred_element_type=F32)
          for g in G]
    y0 = [jnp.dot(m_r[g], uv[g], preferred_element_type=F32) for g in G]
    p_last = [ep[g][L - 1:L, :] for g in G]
    g_mat = [(eye + lax.dot_general(w_b[g], bt[g], _TN, preferred_element_type=F32)) * p_last[g] for g in G]
    h_mat = [lax.dot_general(uv[g], bk[g], _TN, preferred_element_type=F32) * p_last[g] for g in G]
    st = [state[g] for g in G]
    y = [lax.dot_general(rp[g].astype(BF16), st[g].astype(BF16), _NT, preferred_element_type=F32) + y0[g]
         for g in G]
    for g in G:
        state[g] = bdot(st[g], g_mat[g]) + h_mat[g]
    for g in G:
        mu = jnp.mean(y[g], axis=-1, keepdims=True)
        yc = y[g] - mu
        var = jnp.mean(yc * yc, axis=-1, keepdims=True)
        yn = yc * lax.rsqrt(var + GN_EPS) * lnw_ref[g] + lnb_ref[g]
        bonus = jnp.sum(r[g] * k[g] * rk_ref[g], axis=-1, keepdims=True) * v[g]
        y_ref[g] = yn + bonus


def _rwkv_scan(r, lw, k2, v, kkr, ic, rk, lnw, lnb):
    BH, S, N = r.shape
    G = SCAN_HEADS_PER_STEP
    L = SCAN_CHUNK
    groups_per_batch = RWKV_HEADS // G
    blk = pl.BlockSpec((G, L, N), lambda i, c: (i, c, 0))
    par = pl.BlockSpec((G, 1, N), lambda i, c: (i % groups_per_batch, 0, 0))
    return pl.pallas_call(
        _rwkv_scan_kernel,
        grid=(BH // G, S // L),
        in_specs=[blk] * 6 + [par] * 3,
        out_specs=blk,
        out_shape=jax.ShapeDtypeStruct((BH, S, N), F32),
        scratch_shapes=[pltpu.VMEM((G, N, N), F32)],
        compiler_params=_cparams("parallel", "arbitrary"),
        name="rwkv_scan",
    )(r, lw, k2, v, kkr, ic, rk, lnw, lnb)


def _rope_kernel(q_ref, k_ref, v_ref, pos_ref, qo_ref, ko_ref, vo_ref):
    tm = q_ref.shape[0]
    lane = lax.broadcasted_iota(jnp.int32, (1, LANES), 1)
    d = lane % DIFF_QK
    half = ROT_DIMS // 2
    freq = jnp.exp((d % half).astype(F32) * (-2.0 * math.log(ROPE_THETA) / ROT_DIMS))
    freq = jnp.where(d < ROT_DIMS, freq, 0.0)
    ang = pos_ref[...].astype(F32) * freq
    cos = jnp.cos(ang)
    sin = jnp.sin(ang)
    c_mul = jnp.where(d < ROT_DIMS, cos, 1.0)
    s_lo = jnp.where(d < half, -sin, 0.0)
    s_hi = jnp.where((d >= half) & (d < ROT_DIMS), sin, 0.0)
    del tm

    def rot(t):
        return t * c_mul + pltpu.roll(t, LANES - half, 1) * s_lo + pltpu.roll(t, half, 1) * s_hi

    scale = DIFF_QK ** -0.5
    for h in range(DIFF_HEADS):
        sl = slice(h * LANES, (h + 1) * LANES)
        qo_ref[:, sl] = (rot(q_ref[:, sl]) * scale).astype(BF16)
        ko_ref[:, sl] = rot(k_ref[:, sl]).astype(BF16)
    vo_ref[...] = v_ref[...].astype(BF16)


def _rope(main, pos):
    T = main.shape[0]
    tm = min(256, T)
    C = DIFF_WIDTH
    base = 3 * RWKV_WIDTH // C
    blk = lambda c: pl.BlockSpec((tm, C), lambda i: (i, base + c))
    out = pl.BlockSpec((tm, C), lambda i: (i, 0))
    return pl.pallas_call(
        _rope_kernel,
        grid=(T // tm,),
        in_specs=[blk(0), blk(1), blk(2), pl.BlockSpec((tm, 1), lambda i: (i, 0))],
        out_specs=[out] * 3,
        out_shape=[jax.ShapeDtypeStruct((T, C), BF16)] * 3,
        compiler_params=_cparams("parallel"),
        name="rope",
    )(main, main, main, pos)


def _flash_kernel(q_ref, k_ref, v_ref, lq1_ref, lk1_ref, lq2_ref, lk2_ref, sub_ref, o_ref, *, lam_init, tq):
    i = pl.program_id(2)
    tk = tq
    q = q_ref[...]
    lane = lax.broadcasted_iota(jnp.int32, q.shape, 1)
    zero = jnp.zeros_like(q)
    qq = jnp.concatenate([jnp.where(lane < DIFF_QK, q, zero), jnp.where(lane >= DIFF_QK, q, zero)], axis=0)

    def block(j, carry, diagonal):
        m_prev, l_prev, acc = carry
        off = pl.multiple_of(j * tk, tk)
        k = k_ref[pl.ds(off, tk), :]
        v = v_ref[pl.ds(off, tk), :]
        s = lax.dot_general(qq, k, _NT, preferred_element_type=F32)
        if diagonal:
            row = lax.broadcasted_iota(jnp.int32, (2 * tq, tk), 0)
            col = lax.broadcasted_iota(jnp.int32, (2 * tq, tk), 1)
            s = jnp.where(col <= jnp.where(row >= tq, row - tq, row), s, -jnp.inf)
        m_new = jnp.maximum(m_prev, jnp.max(s, axis=1, keepdims=True))
        a = jnp.exp(m_prev - m_new)
        p = jnp.exp(s - m_new)
        l_new = a * l_prev + jnp.sum(p, axis=1, keepdims=True)
        acc = a * acc + jnp.dot(p.astype(BF16), v, preferred_element_type=F32)
        return m_new, l_new, acc

    init = (jnp.full((2 * tq, 1), -jnp.inf, F32), jnp.zeros((2 * tq, 1), F32), jnp.zeros((2 * tq, LANES), F32))
    carry = lax.fori_loop(0, i, lambda j, c: block(j, c, False), init)
    _, l_fin, acc = block(i, carry, True)
    lam = (jnp.exp(jnp.sum(lq1_ref[...] * lk1_ref[...], keepdims=True))
           - jnp.exp(jnp.sum(lq2_ref[...] * lk2_ref[...], keepdims=True)) + lam_init)
    on = acc / l_fin
    o = on[:tq] - lam * on[tq:]
    o = o * lax.rsqrt(jnp.mean(o * o, axis=-1, keepdims=True) + LN_EPS) * sub_ref[...]
    o_ref[...] = o * (1.0 - lam_init)


def _diff_attention(q, k, v, B, S, lq1, lk1, lq2, lk2, subw, layer_idx):
    T = q.shape[0]
    tq = min(512, S)
    nq = S // tq
    lam_init = 0.8 - 0.6 * math.exp(-0.3 * layer_idx)
    vec = pl.BlockSpec((1, DIFF_QK), lambda b, h, i: (0, 0))
    seq = pl.BlockSpec((S, LANES), lambda b, h, i: (b, h))
    return pl.pallas_call(
        functools.partial(_flash_kernel, lam_init=lam_init, tq=tq),
        grid=(B, DIFF_HEADS, nq),
        in_specs=[pl.BlockSpec((tq, LANES), lambda b, h, i: (b * nq + i, h)), seq, seq,
                  vec, vec, vec, vec,
                  pl.BlockSpec((1, DIFF_VHEAD), lambda b, h, i: (0, 0))],
        out_specs=pl.BlockSpec((tq, LANES), lambda b, h, i: (b * nq + i, h)),
        out_shape=jax.ShapeDtypeStruct((T, DIFF_WIDTH), F32),
        compiler_params=_cparams("parallel", "parallel", "parallel"),
        name="diff_attention",
    )(q, k, v, lq1, lk1, lq2, lk2, subw)


def _outproj_kernel(x_ref, ya_ref, g_ref, yb_ref, wa_ref, wb_ref, lnw_ref, lnb_ref, o_ref):
    ya = (ya_ref[...] * g_ref[...]).astype(BF16)
    mix = (jnp.dot(ya, wa_ref[...], preferred_element_type=F32)
           + jnp.dot(yb_ref[...].astype(BF16), wb_ref[...], preferred_element_type=F32))
    o_ref[...] = _layer_norm(DEEPNORM_ALPHA * x_ref[...] + mix, lnw_ref[...], lnb_ref[...])


def _outproj(x, ya, g, yb, wa, wb, lnw, lnb):
    T, D = x.shape
    tm = min(256, T)
    half = pl.BlockSpec((tm, RWKV_WIDTH), lambda i: (i, 0))
    full = pl.BlockSpec((tm, D), lambda i: (i, 0))
    wsp = pl.BlockSpec((RWKV_WIDTH, D), lambda i: (0, 0))
    vec = pl.BlockSpec((1, D), lambda i: (0, 0))
    return pl.pallas_call(
        _outproj_kernel,
        grid=(T // tm,),
        in_specs=[full, half, half, half, wsp, wsp, vec, vec],
        out_specs=full,
        out_shape=jax.ShapeDtypeStruct((T, D), F32),
        compiler_params=_cparams("parallel"),
        name="outproj_ln",
    )(x, ya, g, yb, wa, wb, lnw, lnb)


def _xattn_kernel(x_ref, wq_ref, k_ref, v_ref, wo_ref, lnw_ref, lnb_ref, o_ref):
    x = x_ref[...]
    q = jnp.dot(x.astype(BF16), wq_ref[...], preferred_element_type=F32) * (XATTN_HEAD ** -0.5)
    outs = []
    for h in range(XATTN_HEADS):
        sl = slice(h * XATTN_HEAD, (h + 1) * XATTN_HEAD)
        s = lax.dot_general(q[:, sl].astype(BF16), k_ref[:, sl], _NT, preferred_element_type=F32)
        p = jnp.exp(s - jnp.max(s, axis=-1, keepdims=True))
        p = p / jnp.sum(p, axis=-1, keepdims=True)
        outs.append(jnp.dot(p.astype(BF16), v_ref[:, sl], preferred_element_type=F32))
    o = jnp.concatenate(outs, axis=1).astype(BF16)
    xa = jnp.dot(o, wo_ref[...], preferred_element_type=F32)
    o_ref[...] = _layer_norm(DEEPNORM_ALPHA * x + xa, lnw_ref[...], lnb_ref[...])


def _xattn(x, S, wq, kmem, vmem, wo, lnw, lnb, mem_len):
    T, D = x.shape
    tm = min(256, S)
    tiles_per_seq = S // tm
    full = pl.BlockSpec((tm, D), lambda i: (i, 0))
    kv = pl.BlockSpec((mem_len, XATTN_WIDTH), lambda i: (i // tiles_per_seq, 0))
    vec = pl.BlockSpec((1, D), lambda i: (0, 0))
    return pl.pallas_call(
        _xattn_kernel,
        grid=(T // tm,),
        in_specs=[full, pl.BlockSpec((D, XATTN_WIDTH), lambda i: (0, 0)), kv, kv,
                  pl.BlockSpec((XATTN_WIDTH, D), lambda i: (0, 0)), vec, vec],
        out_specs=full,
        out_shape=jax.ShapeDtypeStruct((T, D), F32),
        compiler_params=_cparams("parallel"),
        name="xattn_ln",
    )(x, wq, kmem, vmem, wo, lnw, lnb)


def _route_kernel(x_ref, pq_ref, sk_ref, e_ref, g_ref, q_s, sv_s, si_s, cand_s, eid_s):
    tm = x_ref.shape[0]
    K = PEER_TOPK
    NK = PEER_KEYS
    q = jnp.dot(x_ref[...].astype(BF16), pq_ref[...], preferred_element_type=F32)
    for hc in range(2 * PEER_HEADS):
        q_s[hc] = q[:, hc * LANES:(hc + 1) * LANES].astype(BF16)
    neg = -jnp.inf
    n_b = [K // (a + 1) for a in range(K)]
    offs = [sum(n_b[:a]) for a in range(K)]
    n_cand = cand_s.shape[0]

    for sub in range(tm // LANES):
        tok = slice(sub * LANES, (sub + 1) * LANES)
        rowk = lax.broadcasted_iota(jnp.int32, (NK, LANES), 0).astype(F32)
        rowc = lax.broadcasted_iota(jnp.int32, (n_cand, LANES), 0).astype(F32)

        def head(h, carry):
            for c in range(2):
                qhc = q_s[2 * h + c, tok, :]
                s = lax.dot_general(sk_ref[c], qhc, _NT, preferred_element_type=F32)
                for j in range(K):
                    m = jnp.max(s, axis=0, keepdims=True)
                    idx = jnp.min(jnp.where(s == m, rowk, float(NK)), axis=0, keepdims=True)
                    s = jnp.where(rowk == idx, neg, s)
                    sv_s[c, j:j + 1, :] = m
                    si_s[c, j:j + 1, :] = idx
            cand_s[n_cand - 8:, :] = jnp.full((8, LANES), neg, F32)
            eid_s[n_cand - 8:, :] = jnp.zeros((8, LANES), F32)
            for a in range(K):
                rows = slice(offs[a], offs[a] + n_b[a])
                cand_s[rows, :] = sv_s[0, a:a + 1, :] + sv_s[1, 0:n_b[a], :]
                eid_s[rows, :] = si_s[0, a:a + 1, :] * float(NK) + si_s[1, 0:n_b[a], :]
            cand = cand_s[...]
            eid = eid_s[...]
            fv, fe = [], []
            for j in range(K):
                m = jnp.max(cand, axis=0, keepdims=True)
                idx = jnp.min(jnp.where(cand == m, rowc, float(n_cand)), axis=0, keepdims=True)
                sel = rowc == idx
                fe.append(jnp.max(jnp.where(sel, eid, -1.0), axis=0, keepdims=True))
                cand = jnp.where(sel, neg, cand)
                fv.append(m)
            w = [jnp.exp(fv[j] - fv[0]) for j in range(K)]
            den = w[0]
            for j in range(1, K):
                den = den + w[j]
            for j in range(K):
                sv_s[0, j:j + 1, :] = w[j] / den
                si_s[0, j:j + 1, :] = fe[j]
            base = pl.multiple_of(h * K, K)
            g_ref[pl.ds(base, K), tok] = sv_s[0]
            e_ref[pl.ds(base, K), tok] = si_s[0].astype(jnp.int32)
            return carry

        lax.fori_loop(0, PEER_HEADS, head, 0)


def _route(x, pq, sk):
    T, D = x.shape
    tm = min(256, T)
    out = pl.BlockSpec((PEER_SLOTS, tm), lambda i: (0, i))
    n_cand = sum(PEER_TOPK // (a + 1) for a in range(PEER_TOPK))
    n_cand = -(-n_cand // 8) * 8
    return pl.pallas_call(
        _route_kernel,
        grid=(T // tm,),
        in_specs=[pl.BlockSpec((tm, D), lambda i: (i, 0)),
                  pl.BlockSpec((D, D), lambda i: (0, 0)),
                  pl.BlockSpec((2, PEER_KEYS, LANES), lambda i: (0, 0, 0))],
        out_specs=[out, out],
        out_shape=[jax.ShapeDtypeStruct((PEER_SLOTS, T), jnp.int32),
                   jax.ShapeDtypeStruct((PEER_SLOTS, T), F32)],
        scratch_shapes=[pltpu.VMEM((2 * PEER_HEADS, tm, LANES), BF16),
                        pltpu.VMEM((2, PEER_TOPK, LANES), F32),
                        pltpu.VMEM((2, PEER_TOPK, LANES), F32),
                        pltpu.VMEM((n_cand, LANES), F32),
                        pltpu.VMEM((n_cand, LANES), F32)],
        compiler_params=_cparams("parallel"),
        name="peer_route",
    )(x, pq, sk)


EXPERT_TOKENS_PER_STEP = 128
EXPERT_BUFS = 4
EXPERT_LOOKAHEAD = EXPERT_BUFS - 1


def _expert_kernel(idx_ref, nidx_ref, x_ref, gt_ref, lnw_ref, lnb_ref, uv_ref, o_ref, *scratch):
    bufs, sem = scratch[:EXPERT_BUFS], scratch[EXPERT_BUFS]
    TB = x_ref.shape[0]
    NE = PEER_SLOTS
    step = pl.program_id(0)
    per_chunk = NE // (2 * D_CHUNKS)

    def start(iref, t, e, b):
        pltpu.make_async_copy(uv_ref.at[iref[e, t]], bufs[b].at[:, e, :], sem.at[b]).start()

    def wait(b):
        pltpu.make_async_copy(bufs[b], bufs[b], sem.at[b]).wait()

    @pl.when(step == 0)
    def _():
        for b in range(EXPERT_LOOKAHEAD):
            def body(e, c, b=b):
                start(idx_ref, b, e, b)
                return c
            lax.fori_loop(0, NE, body, 0)

    lane = lax.broadcasted_iota(jnp.int32, (NE, TB), 1)

    def token(t, b, issue):
        buf = bufs[b]
        wait(b)
        xrow = x_ref[pl.ds(t, 1), :]
        acc = jnp.zeros((NE, LANES), F32)
        for s in range(D_CHUNKS):
            for e in range(s * per_chunk, (s + 1) * per_chunk):
                issue(e)
            u = lax.bitcast_convert_type(buf[s] & jnp.uint32(0xFFFF0000), F32)
            acc = acc + u * xrow[:, s * LANES:(s + 1) * LANES]
        h = jnp.sum(acc, axis=1, keepdims=True)
        gate = jnp.sum(jnp.where(lane == t, gt_ref[...], 0.0), axis=1, keepdims=True)
        a = 0.5 * h * (1.0 + lax.erf(h * (2.0 ** -0.5))) * gate
        ys = []
        for s in range(D_CHUNKS):
            for e in range((D_CHUNKS + s) * per_chunk, (D_CHUNKS + s + 1) * per_chunk):
                issue(e)
            v = lax.bitcast_convert_type(buf[s] << 16, F32)
            ys.append(jnp.sum(v * a, axis=0, keepdims=True))
        z = DEEPNORM_ALPHA * xrow + jnp.concatenate(ys, axis=1)
        o_ref[t] = _layer_norm(z, lnw_ref[...], lnb_ref[...])

    def ring(t0, last):
        for b in range(EXPERT_BUFS):
            nb = (b + EXPERT_LOOKAHEAD) % EXPERT_BUFS
            ahead = b + EXPERT_LOOKAHEAD
            if last and ahead >= EXPERT_BUFS:
                issue = lambda e, tn=ahead - EXPERT_BUFS, nb=nb: start(nidx_ref, tn, e, nb)
            else:
                issue = lambda e, tn=t0 + ahead, nb=nb: start(idx_ref, tn, e, nb)
            token(t0 + b, b, issue)

    def ring_body(q, c):
        ring(q * EXPERT_BUFS, False)
        return c

    lax.fori_loop(0, TB // EXPERT_BUFS - 1, ring_body, 0)
    ring(TB - EXPERT_BUFS, True)

    @pl.when(step == pl.num_programs(0) - 1)
    def _():
        for b in range(EXPERT_LOOKAHEAD):
            wait(b)


def _experts(exp_t, x, gates_t, lnw, lnb, uv):
    T, D = x.shape
    TB = min(EXPERT_TOKENS_PER_STEP, T)
    NE = PEER_SLOTS
    n = T // TB
    out = pl.pallas_call(
        _expert_kernel,
        grid=(n,),
        in_specs=[pl.BlockSpec((NE, TB), lambda i: (0, i), memory_space=pltpu.SMEM),
                  pl.BlockSpec((NE, TB), lambda i: (0, jnp.minimum(i + 1, n - 1)), memory_space=pltpu.SMEM),
                  pl.BlockSpec((TB, D), lambda i: (i, 0)),
                  pl.BlockSpec((NE, TB), lambda i: (0, i)),
                  pl.BlockSpec((1, D), lambda i: (0, 0)),
                  pl.BlockSpec((1, D), lambda i: (0, 0)),
                  pl.BlockSpec(memory_space=pl.ANY)],
        out_specs=pl.BlockSpec((TB, 1, D), lambda i: (i, 0, 0)),
        out_shape=jax.ShapeDtypeStruct((T, 1, D), F32),
        scratch_shapes=[pltpu.VMEM((D_CHUNKS, NE, LANES), jnp.uint32)] * EXPERT_BUFS
        + [pltpu.SemaphoreType.DMA((EXPERT_BUFS,))],
        compiler_params=_cparams("arbitrary"),
        name="peer_experts",
    )(exp_t, exp_t, x, gates_t, lnw, lnb, uv)
    return out.reshape(T, D)


def kernel(x, mem, positions, w_in, shift_mu, w0, w_up, a0, a_up, g_up, k_k, k_a, r_k, lnx_w, lnx_b, lam_q1, lam_k1, lam_q2, lam_k2, subln_w, w_out, ln1_w, ln1_b, xq, xk, xv, xo, ln2_w, ln2_b, pq, subkeys, peer_u, peer_v, ln3_w, ln3_b):
    B, S, D = x.shape
    T = B * S
    M = mem.shape[1]
    H, N, C = RWKV_HEADS, RWKV_HEAD, RWKV_WIDTH
    xt = x.reshape(T, D)
    memt = mem.reshape(B * M, D)
    pos = positions.reshape(T, 1)
    row = lambda a: a.reshape(1, -1)
    to_heads = lambda a: a.reshape(B, S, H, N).transpose(0, 2, 1, 3).reshape(B * H, S, N)
    lora_pad = LORA_PAD - LORA_COLS

    for l in range(DEPTH):
        wl = w_in[l]
        w_main = jnp.concatenate([wl[:, :3 * C], wl[:, RWKV_COLS:]], axis=1).astype(BF16)
        w_lora = jnp.pad(wl[:, 3 * C:RWKV_COLS], ((0, 0), (0, lora_pad))).astype(BF16)
        mu = shift_mu[l]
        mu_l = jnp.pad(mu[3 * C:], (0, lora_pad))
        o1, o2 = DECAY_LORA, DECAY_LORA + AAA_LORA
        wup = jnp.pad(w_up[l], ((0, LORA_PAD - o1), (0, 0))).astype(BF16)
        aup = jnp.pad(a_up[l], ((o1, LORA_PAD - o2), (0, 0))).astype(BF16)
        gup = jnp.pad(g_up[l], ((o2, lora_pad), (0, 0))).astype(BF16)

        main = _matmul(xt, w_main, F32, 512, 1024)
        lora = _matmul(xt, w_lora, F32, 512, LORA_PAD)
        r, lw, k2, v, kkr, ic, g = _rwkv_prep(
            main, lora, S, row(mu[:C]), row(mu[C:2 * C]), row(mu[2 * C:3 * C]), row(mu_l),
            row(w0[l]), row(a0[l]), row(k_k[l]), row(k_a[l]), wup, aup, gup)
        y = _rwkv_scan(to_heads(r), to_heads(lw), to_heads(k2), to_heads(v), to_heads(kkr), to_heads(ic),
                       r_k[l].reshape(H, 1, N), lnx_w[l].reshape(H, 1, N), lnx_b[l].reshape(H, 1, N))
        ya = y.reshape(B, H, S, N).transpose(0, 2, 1, 3).reshape(T, C)

        qr, kr, vb = _rope(main, pos)
        yb = _diff_attention(qr, kr, vb, B, S, row(lam_q1[l]), row(lam_k1[l]), row(lam_q2[l]), row(lam_k2[l]),
                             row(subln_w[l]), l)

        wo = w_out[l].astype(BF16)
        x1 = _outproj(xt, ya, g, yb, wo[:C], wo[C:], row(ln1_w[l]), row(ln1_b[l]))

        kmem = _matmul(memt, xk[l].astype(BF16), BF16, 512, XATTN_WIDTH)
        vmem = _matmul(memt, xv[l].astype(BF16), BF16, 512, XATTN_WIDTH)
        x2 = _xattn(x1, S, xq[l].astype(BF16), kmem, vmem, xo[l].astype(BF16), row(ln2_w[l]), row(ln2_b[l]), M)

        exp_t, gates_t = _route(x2, pq[l].astype(BF16), subkeys[l].astype(BF16))
        half = lambda w: lax.bitcast_convert_type(w.astype(BF16), jnp.uint16).astype(jnp.uint32)
        uv = ((half(peer_u[l]) << 16) | half(peer_v[l])).reshape(PEER_EXPERTS, D_CHUNKS, LANES)
        xt = _experts(exp_t, x2, gates_t, row(ln3_w[l]), row(ln3_b[l]), uv)
    return xt.reshape(B, S, D)
```

```python
import functools
import math

import jax
import jax.numpy as jnp
from jax import lax
from jax.experimental import pallas as pl
from jax.experimental.pallas import tpu as pltpu

F32 = jnp.float32
BF16 = jnp.bfloat16

D_MODEL = 2048
DEPTH = 2
RWKV_WIDTH = 1024
RWKV_HEAD = 64
RWKV_HEADS = 16
DECAY_LORA = 64
AAA_LORA = 64
GATE_LORA = 160
LORA_COLS = DECAY_LORA + AAA_LORA + GATE_LORA
LORA_PAD = 384
RWKV_COLS = 3 * RWKV_WIDTH + LORA_COLS
DIFF_WIDTH = 1024
DIFF_VHEAD = 128
DIFF_HEADS = 8
DIFF_QK = 64
ROT_DIMS = 16
ROPE_THETA = 500000.0
XATTN_HEADS = 4
XATTN_HEAD = 128
XATTN_WIDTH = 512
PEER_HEADS = 8
PEER_KEYS = 128
PEER_EXPERTS = PEER_KEYS * PEER_KEYS
PEER_TOPK = 16
PEER_SLOTS = PEER_HEADS * PEER_TOPK
LN_EPS = 1e-5
GN_EPS = 64e-5
DEEPNORM_ALPHA = (2.0 * DEPTH) ** 0.25

LANES = 128
D_CHUNKS = D_MODEL // LANES
SCAN_CHUNK = 64
SCAN_HEADS_PER_STEP = 8
SCAN_CHUNKS_PER_STEP = 2
VMEM_LIMIT = 48 * 1024 * 1024

_NT = (((1,), (1,)), ((), ()))
_TN = (((0,), (0,)), ((), ()))


def _cparams(*sem):
    return pltpu.CompilerParams(dimension_semantics=sem, vmem_limit_bytes=VMEM_LIMIT)


def _layer_norm(z, w, b):
    mu = jnp.mean(z, axis=-1, keepdims=True)
    zc = z - mu
    var = jnp.mean(zc * zc, axis=-1, keepdims=True)
    return zc * lax.rsqrt(var + LN_EPS) * w + b


def _mm_kernel(a_ref, w_ref, o_ref):
    o_ref[...] = jnp.dot(a_ref[...].astype(BF16), w_ref[...],
                         preferred_element_type=F32).astype(o_ref.dtype)


def _matmul(a, w, out_dtype, tm, tn):
    M, K = a.shape
    N = w.shape[1]
    tm, tn = min(tm, M), min(tn, N)
    return pl.pallas_call(
        _mm_kernel,
        grid=(N // tn, M // tm),
        in_specs=[pl.BlockSpec((tm, K), lambda j, i: (i, 0)),
                  pl.BlockSpec((K, tn), lambda j, i: (0, j))],
        out_specs=pl.BlockSpec((tm, tn), lambda j, i: (i, j)),
        out_shape=jax.ShapeDtypeStruct((M, N), out_dtype),
        compiler_params=_cparams("parallel", "parallel"),
        name="matmul",
    )(a, w)


def _rwkv_prep_kernel(r_ref, k_ref, v_ref, l_ref, rp_ref, kp_ref, vp_ref, lp_ref,
                      mur_ref, muk_ref, muv_ref, mul_ref, w0_ref, a0_ref, kk_ref, ka_ref,
                      wup_ref, aup_ref, gup_ref,
                      ro_ref, lw_ref, k2_ref, vo_ref, kkr_ref, ic_ref, g_ref, *, tiles_per_seq):
    i = pl.program_id(0)
    first = (i % tiles_per_seq) == 0

    def shifted(cur_ref, prev_ref, mu_ref):
        p = cur_ref[...]
        tm = p.shape[0]
        prev_row = jnp.where(first, 0.0, prev_ref[7:8, :])
        row = lax.broadcasted_iota(jnp.int32, p.shape, 0)
        p_prev = jnp.where(row == 0, prev_row, pltpu.roll(p, 1, 0))
        del tm
        return p + (p_prev - p) * mu_ref[...]

    r = shifted(r_ref, rp_ref, mur_ref)
    k = shifted(k_ref, kp_ref, muk_ref)
    v = shifted(v_ref, vp_ref, muv_ref)
    lo = shifted(l_ref, lp_ref, mul_ref)

    wl = w0_ref[...] + jnp.dot(jnp.tanh(lo).astype(BF16), wup_ref[...], preferred_element_type=F32)
    nz = -wl
    softplus = jnp.maximum(nz, 0.0) + jnp.log(1.0 + jnp.exp(-jnp.abs(nz)))
    w_log = -softplus - 0.5
    lw_ref[...] = -jnp.exp(w_log)
    al = a0_ref[...] + jnp.dot(lo.astype(BF16), aup_ref[...], preferred_element_type=F32)
    iclr = 1.0 / (1.0 + jnp.exp(-al))
    sg = 1.0 / (1.0 + jnp.exp(-lo))
    g_ref[...] = jnp.dot(sg.astype(BF16), gup_ref[...], preferred_element_type=F32)
    ro_ref[...] = r
    vo_ref[...] = v
    kkr_ref[...] = k * kk_ref[...]
    ic_ref[...] = iclr
    k2_ref[...] = k * (1.0 + (iclr - 1.0) * ka_ref[...])


def _rwkv_prep(main, lora, S, mu_r, mu_k, mu_v, mu_l, w0, a0, k_k, k_a, wup, aup, gup):
    T = main.shape[0]
    tm = min(256, S)
    C = RWKV_WIDTH
    prev = lambda c: (lambda i: (jnp.maximum(i * (tm // 8) - 1, 0), c))
    cur = lambda c: (lambda i: (i, c))
    vec = lambda n: pl.BlockSpec((1, n), lambda i: (0, 0))
    mat = pl.BlockSpec((LORA_PAD, C), lambda i: (0, 0))
    out = pl.BlockSpec((tm, C), lambda i: (i, 0))
    return pl.pallas_call(
        functools.partial(_rwkv_prep_kernel, tiles_per_seq=S // tm),
        grid=(T // tm,),
        in_specs=[pl.BlockSpec((tm, C), cur(0)), pl.BlockSpec((tm, C), cur(1)), pl.BlockSpec((tm, C), cur(2)),
                  pl.BlockSpec((tm, LORA_PAD), cur(0)),
                  pl.BlockSpec((8, C), prev(0)), pl.BlockSpec((8, C), prev(1)), pl.BlockSpec((8, C), prev(2)),
                  pl.BlockSpec((8, LORA_PAD), prev(0)),
                  vec(C), vec(C), vec(C), vec(LORA_PAD), vec(C), vec(C), vec(C), vec(C),
                  mat, mat, mat],
        out_specs=[out] * 7,
        out_shape=[jax.ShapeDtypeStruct((T, C), F32)] * 7,
        compiler_params=_cparams("parallel"),
        name="rwkv_prep",
    )(main, main, main, lora, main, main, main, lora,
      mu_r, mu_k, mu_v, mu_l, w0, a0, k_k, k_a, wup, aup, gup)


def _rwkv_scan_kernel(r_ref, lw_ref, k_ref, v_ref, kkr_ref, ic_ref, rk_ref, lnw_ref, lnb_ref,
                      y_ref, state):
    L = SCAN_CHUNK
    N = RWKV_HEAD
    win = lambda ref, q: ref[q[1] * L:(q[1] + 1) * L, q[0] * N:(q[0] + 1) * N]

    @pl.when(pl.program_id(1) == 0)
    def _():
        state[...] = jnp.zeros_like(state)

    row = lax.broadcasted_iota(jnp.int32, (L, L), 0)
    col = lax.broadcasted_iota(jnp.int32, (L, L), 1)
    tril_incl = (col <= row).astype(F32)
    eye = (col == row).astype(F32)
    row2 = lax.broadcasted_iota(jnp.int32, (L, 2 * L), 0)
    col2 = lax.broadcasted_iota(jnp.int32, (L, 2 * L), 1) % L
    zeros_ln = jnp.zeros((L, N), BF16)
    bdot = lambda a, b: jnp.dot(a.astype(BF16), b.astype(BF16), preferred_element_type=F32)

    heads = SCAN_HEADS_PER_STEP
    pairs = [(hd, c) for c in range(SCAN_CHUNKS_PER_STEP) for hd in range(heads)]
    G = range(len(pairs))
    r = [win(r_ref, q) for q in pairs]
    lw = [win(lw_ref, q) for q in pairs]
    k = [win(k_ref, q) for q in pairs]
    v = [win(v_ref, q) for q in pairs]
    lp = [jnp.dot(tril_incl, lw[g], precision=lax.Precision.HIGHEST, preferred_element_type=F32) for g in G]
    ep = [jnp.exp(lp[g]) for g in G]
    en = [jnp.exp(-lp[g]) for g in G]
    kk = []
    for q in pairs:
        kkr = win(kkr_ref, q)
        nrm = jnp.sqrt(jnp.sum(kkr * kkr, axis=-1, keepdims=True))
        kk.append(kkr / jnp.maximum(nrm, 1e-12))
    at = [((-kk[g]) * jnp.exp(lp[g] - lw[g])).astype(BF16) for g in G]
    rt = [r[g] * ep[g] for g in G]
    bt = [((kk[g] * win(ic_ref, pairs[g])) * en[g]).astype(BF16) for g in G]
    bk = [jnp.concatenate([bt[g], (k[g] * en[g]).astype(BF16)], axis=0) for g in G]
    vb = [v[g].astype(BF16) for g in G]
    m_a = [jnp.where(col2 < row2, lax.dot_general(at[g], bk[g], _NT, preferred_element_type=F32), 0.0) for g in G]
    m_r = [jnp.where(col2 <= row2, lax.dot_general(rt[g].astype(BF16), bk[g], _NT, preferred_element_type=F32),
                     0.0).astype(BF16) for g in G]
    x = [jnp.dot(m_a[g].astype(BF16), jnp.concatenate([zeros_ln, vb[g]], axis=0), preferred_element_type=F32)
         for g in G]
    square = lambda m: [bdot(m[g], m[g]) for g in G]
    pair = lambda lo, hi: [(eye + lo[g]) + bdot(eye + lo[g], hi[g]) for g in G]
    p1 = [m_a[g][:, :L] for g in G]
    p2 = square(p1)
    f01 = pair(p1, p2)
    p4 = square(p2)
    p8 = square(p4)
    f23 = pair(p4, p8)
    f03 = [bdot(f01[g], f23[g]) for g in G]
    p16 = square(p8)
    p32 = square(p16)
    f45 = pair(p16, p32)
    tb = [bdot(f03[g], f45[g]).astype(BF16) for g in G]
    w_b = [jnp.dot(tb[g], at[g], preferred_element_type=F32).astype(BF16) for g in G]
    u0_b = [jnp.dot(tb[g], x[g].astype(BF16), preferred_element_type=F32).astype(BF16) for g in G]
    uv = [jnp.concatenate([u0_b[g], vb[g]], axis=0) for g in G]
    rp = [rt[g] + jnp.dot(m_r[g], jnp.concatenate([w_b[g], zeros_ln], axis=0), preferred_element_type=F32)
          for g in G]
    y0 = [jnp.dot(m_r[g], uv[g], preferred_element_type=F32) for g in G]
    p_last = [ep[g][L - 1:L, :] for g in G]
    g_mat = [(eye + lax.dot_general(w_b[g], bt[g], _TN, preferred_element_type=F32)) * p_last[g] for g in G]
    h_mat = [lax.dot_general(uv[g], bk[g], _TN, preferred_element_type=F32) * p_last[g] for g in G]
    st = [state[hd] for hd in range(heads)]
    y = [None] * len(pairs)
    for g in G:
        hd = pairs[g][0]
        y[g] = lax.dot_general(rp[g].astype(BF16), st[hd].astype(BF16), _NT, preferred_element_type=F32) + y0[g]
        st[hd] = bdot(st[hd], g_mat[g]) + h_mat[g]
    for hd in range(heads):
        state[hd] = st[hd]
    for g in G:
        hd, c = pairs[g]
        mu = jnp.mean(y[g], axis=-1, keepdims=True)
        yc = y[g] - mu
        var = jnp.mean(yc * yc, axis=-1, keepdims=True)
        yn = yc * lax.rsqrt(var + GN_EPS) * lnw_ref[hd] + lnb_ref[hd]
        bonus = jnp.sum(r[g] * k[g] * rk_ref[hd], axis=-1, keepdims=True) * v[g]
        y_ref[c * L:(c + 1) * L, hd * N:(hd + 1) * N] = yn + bonus


def _rwkv_scan(r, lw, k2, v, kkr, ic, rk, lnw, lnb, S):
    T = r.shape[0]
    N = RWKV_HEAD
    G = SCAN_HEADS_PER_STEP
    rows = SCAN_CHUNK * SCAN_CHUNKS_PER_STEP
    groups_per_batch = RWKV_HEADS // G
    steps = S // rows
    blk = pl.BlockSpec((rows, G * N), lambda i, c: (i // groups_per_batch * steps + c, i % groups_per_batch))
    par = pl.BlockSpec((G, 1, N), lambda i, c: (i % groups_per_batch, 0, 0))
    return pl.pallas_call(
        _rwkv_scan_kernel,
        grid=(T // S * groups_per_batch, steps),
        in_specs=[blk] * 6 + [par] * 3,
        out_specs=blk,
        out_shape=jax.ShapeDtypeStruct((T, RWKV_WIDTH), F32),
        scratch_shapes=[pltpu.VMEM((G, N, N), F32)],
        compiler_params=_cparams("parallel", "arbitrary"),
        name="rwkv_scan",
    )(r, lw, k2, v, kkr, ic, rk, lnw, lnb)


def _rope_kernel(q_ref, k_ref, v_ref, pos_ref, qo_ref, ko_ref, vo_ref):
    tm = q_ref.shape[0]
    lane = lax.broadcasted_iota(jnp.int32, (1, LANES), 1)
    d = lane % DIFF_QK
    half = ROT_DIMS // 2
    freq = jnp.exp((d % half).astype(F32) * (-2.0 * math.log(ROPE_THETA) / ROT_DIMS))
    freq = jnp.where(d < ROT_DIMS, freq, 0.0)
    ang = pos_ref[...].astype(F32) * freq
    cos = jnp.cos(ang)
    sin = jnp.sin(ang)
    c_mul = jnp.where(d < ROT_DIMS, cos, 1.0)
    s_lo = jnp.where(d < half, -sin, 0.0)
    s_hi = jnp.where((d >= half) & (d < ROT_DIMS), sin, 0.0)
    del tm

    def rot(t):
        return t * c_mul + pltpu.roll(t, LANES - half, 1) * s_lo + pltpu.roll(t, half, 1) * s_hi

    scale = DIFF_QK ** -0.5
    for h in range(DIFF_HEADS):
        sl = slice(h * LANES, (h + 1) * LANES)
        qo_ref[:, sl] = (rot(q_ref[:, sl]) * scale).astype(BF16)
        ko_ref[:, sl] = rot(k_ref[:, sl]).astype(BF16)
    vo_ref[...] = v_ref[...].astype(BF16)


def _rope(main, pos):
    T = main.shape[0]
    tm = min(256, T)
    C = DIFF_WIDTH
    base = 3 * RWKV_WIDTH // C
    blk = lambda c: pl.BlockSpec((tm, C), lambda i: (i, base + c))
    out = pl.BlockSpec((tm, C), lambda i: (i, 0))
    return pl.pallas_call(
        _rope_kernel,
        grid=(T // tm,),
        in_specs=[blk(0), blk(1), blk(2), pl.BlockSpec((tm, 1), lambda i: (i, 0))],
        out_specs=[out] * 3,
        out_shape=[jax.ShapeDtypeStruct((T, C), BF16)] * 3,
        compiler_params=_cparams("parallel"),
        name="rope",
    )(main, main, main, pos)


def _flash_kernel(q_ref, k_ref, v_ref, lq1_ref, lk1_ref, lq2_ref, lk2_ref, sub_ref, o_ref, *, lam_init, tq):
    i = pl.program_id(2)
    tk = tq
    q = q_ref[...]
    lane = lax.broadcasted_iota(jnp.int32, q.shape, 1)
    zero = jnp.zeros_like(q)
    qq = jnp.concatenate([jnp.where(lane < DIFF_QK, q, zero), jnp.where(lane >= DIFF_QK, q, zero)], axis=0)

    def block(j, carry, diagonal):
        m_prev, l_prev, acc = carry
        off = pl.multiple_of(j * tk, tk)
        k = k_ref[pl.ds(off, tk), :]
        v = v_ref[pl.ds(off, tk), :]
        s = lax.dot_general(qq, k, _NT, preferred_element_type=F32)
        if diagonal:
            row = lax.broadcasted_iota(jnp.int32, (2 * tq, tk), 0)
            col = lax.broadcasted_iota(jnp.int32, (2 * tq, tk), 1)
            s = jnp.where(col <= jnp.where(row >= tq, row - tq, row), s, -jnp.inf)
        m_new = jnp.maximum(m_prev, jnp.max(s, axis=1, keepdims=True))
        a = jnp.exp(m_prev - m_new)
        p = jnp.exp(s - m_new)
        l_new = a * l_prev + jnp.sum(p, axis=1, keepdims=True)
        acc = a * acc + jnp.dot(p.astype(BF16), v, preferred_element_type=F32)
        return m_new, l_new, acc

    init = (jnp.full((2 * tq, 1), -jnp.inf, F32), jnp.zeros((2 * tq, 1), F32), jnp.zeros((2 * tq, LANES), F32))
    carry = lax.fori_loop(0, i, lambda j, c: block(j, c, False), init)
    _, l_fin, acc = block(i, carry, True)
    lam = (jnp.exp(jnp.sum(lq1_ref[...] * lk1_ref[...], keepdims=True))
           - jnp.exp(jnp.sum(lq2_ref[...] * lk2_ref[...], keepdims=True)) + lam_init)
    on = acc / l_fin
    o = on[:tq] - lam * on[tq:]
    o = o * lax.rsqrt(jnp.mean(o * o, axis=-1, keepdims=True) + LN_EPS) * sub_ref[...]
    o_ref[...] = o * (1.0 - lam_init)


def _diff_attention(q, k, v, B, S, lq1, lk1, lq2, lk2, subw, layer_idx):
    T = q.shape[0]
    tq = min(512, S)
    nq = S // tq
    lam_init = 0.8 - 0.6 * math.exp(-0.3 * layer_idx)
    vec = pl.BlockSpec((1, DIFF_QK), lambda b, h, i: (0, 0))
    seq = pl.BlockSpec((S, LANES), lambda b, h, i: (b, h))
    return pl.pallas_call(
        functools.partial(_flash_kernel, lam_init=lam_init, tq=tq),
        grid=(B, DIFF_HEADS, nq),
        in_specs=[pl.BlockSpec((tq, LANES), lambda b, h, i: (b * nq + i, h)), seq, seq,
                  vec, vec, vec, vec,
                  pl.BlockSpec((1, DIFF_VHEAD), lambda b, h, i: (0, 0))],
        out_specs=pl.BlockSpec((tq, LANES), lambda b, h, i: (b * nq + i, h)),
        out_shape=jax.ShapeDtypeStruct((T, DIFF_WIDTH), F32),
        compiler_params=_cparams("parallel", "parallel", "parallel"),
        name="diff_attention",
    )(q, k, v, lq1, lk1, lq2, lk2, subw)


def _outproj_kernel(x_ref, ya_ref, g_ref, yb_ref, wa_ref, wb_ref, lnw_ref, lnb_ref, o_ref):
    ya = (ya_ref[...] * g_ref[...]).astype(BF16)
    mix = (jnp.dot(ya, wa_ref[...], preferred_element_type=F32)
           + jnp.dot(yb_ref[...].astype(BF16), wb_ref[...], preferred_element_type=F32))
    o_ref[...] = _layer_norm(DEEPNORM_ALPHA * x_ref[...] + mix, lnw_ref[...], lnb_ref[...])


def _outproj(x, ya, g, yb, wa, wb, lnw, lnb):
    T, D = x.shape
    tm = min(256, T)
    half = pl.BlockSpec((tm, RWKV_WIDTH), lambda i: (i, 0))
    full = pl.BlockSpec((tm, D), lambda i: (i, 0))
    wsp = pl.BlockSpec((RWKV_WIDTH, D), lambda i: (0, 0))
    vec = pl.BlockSpec((1, D), lambda i: (0, 0))
    return pl.pallas_call(
        _outproj_kernel,
        grid=(T // tm,),
        in_specs=[full, half, half, half, wsp, wsp, vec, vec],
        out_specs=full,
        out_shape=jax.ShapeDtypeStruct((T, D), F32),
        compiler_params=_cparams("parallel"),
        name="outproj_ln",
    )(x, ya, g, yb, wa, wb, lnw, lnb)


def _xattn_kernel(x_ref, wq_ref, k_ref, v_ref, wo_ref, lnw_ref, lnb_ref, o_ref):
    x = x_ref[...]
    q = jnp.dot(x.astype(BF16), wq_ref[...], preferred_element_type=F32) * (XATTN_HEAD ** -0.5)
    outs = []
    for h in range(XATTN_HEADS):
        sl = slice(h * XATTN_HEAD, (h + 1) * XATTN_HEAD)
        s = lax.dot_general(q[:, sl].astype(BF16), k_ref[:, sl], _NT, preferred_element_type=F32)
        p = jnp.exp(s - jnp.max(s, axis=-1, keepdims=True))
        p = p / jnp.sum(p, axis=-1, keepdims=True)
        outs.append(jnp.dot(p.astype(BF16), v_ref[:, sl], preferred_element_type=F32))
    o = jnp.concatenate(outs, axis=1).astype(BF16)
    xa = jnp.dot(o, wo_ref[...], preferred_element_type=F32)
    o_ref[...] = _layer_norm(DEEPNORM_ALPHA * x + xa, lnw_ref[...], lnb_ref[...])


def _xattn(x, S, wq, kmem, vmem, wo, lnw, lnb, mem_len):
    T, D = x.shape
    tm = min(256, S)
    tiles_per_seq = S // tm
    full = pl.BlockSpec((tm, D), lambda i: (i, 0))
    kv = pl.BlockSpec((mem_len, XATTN_WIDTH), lambda i: (i // tiles_per_seq, 0))
    vec = pl.BlockSpec((1, D), lambda i: (0, 0))
    return pl.pallas_call(
        _xattn_kernel,
        grid=(T // tm,),
        in_specs=[full, pl.BlockSpec((D, XATTN_WIDTH), lambda i: (0, 0)), kv, kv,
                  pl.BlockSpec((XATTN_WIDTH, D), lambda i: (0, 0)), vec, vec],
        out_specs=full,
        out_shape=jax.ShapeDtypeStruct((T, D), F32),
        compiler_params=_cparams("parallel"),
        name="xattn_ln",
    )(x, wq, kmem, vmem, wo, lnw, lnb)


def _route_kernel(x_ref, pq_ref, sk_ref, e_ref, g_ref, q_s, sv_s, si_s, cand_s, eid_s):
    tm = x_ref.shape[0]
    K = PEER_TOPK
    NK = PEER_KEYS
    q = jnp.dot(x_ref[...].astype(BF16), pq_ref[...], preferred_element_type=F32)
    for hc in range(2 * PEER_HEADS):
        q_s[hc] = q[:, hc * LANES:(hc + 1) * LANES].astype(BF16)
    neg = -jnp.inf
    n_b = [K // (a + 1) for a in range(K)]
    offs = [sum(n_b[:a]) for a in range(K)]
    n_cand = cand_s.shape[0]

    for sub in range(tm // LANES):
        tok = slice(sub * LANES, (sub + 1) * LANES)
        rowk = lax.broadcasted_iota(jnp.int32, (NK, LANES), 0).astype(F32)
        rowc = lax.broadcasted_iota(jnp.int32, (n_cand, LANES), 0).astype(F32)

        def head(h, carry):
            for c in range(2):
                qhc = q_s[2 * h + c, tok, :]
                s = lax.dot_general(sk_ref[c], qhc, _NT, preferred_element_type=F32)
                for j in range(K):
                    m = jnp.max(s, axis=0, keepdims=True)
                    idx = jnp.min(jnp.where(s == m, rowk, float(NK)), axis=0, keepdims=True)
                    s = jnp.where(rowk == idx, neg, s)
                    sv_s[c, j:j + 1, :] = m
                    si_s[c, j:j + 1, :] = idx
            cand_s[n_cand - 8:, :] = jnp.full((8, LANES), neg, F32)
            eid_s[n_cand - 8:, :] = jnp.zeros((8, LANES), F32)
            for a in range(K):
                rows = slice(offs[a], offs[a] + n_b[a])
                cand_s[rows, :] = sv_s[0, a:a + 1, :] + sv_s[1, 0:n_b[a], :]
                eid_s[rows, :] = si_s[0, a:a + 1, :] * float(NK) + si_s[1, 0:n_b[a], :]
            cand = cand_s[...]
            eid = eid_s[...]
            fv, fe = [], []
            for j in range(K):
                m = jnp.max(cand, axis=0, keepdims=True)
                idx = jnp.min(jnp.where(cand == m, rowc, float(n_cand)), axis=0, keepdims=True)
                sel = rowc == idx
                fe.append(jnp.max(jnp.where(sel, eid, -1.0), axis=0, keepdims=True))
                cand = jnp.where(sel, neg, cand)
                fv.append(m)
            w = [jnp.exp(fv[j] - fv[0]) for j in range(K)]
            den = w[0]
            for j in range(1, K):
                den = den + w[j]
            for j in range(K):
                sv_s[0, j:j + 1, :] = w[j] / den
                si_s[0, j:j + 1, :] = fe[j]
            base = pl.multiple_of(h * K, K)
            g_ref[pl.ds(base, K), tok] = sv_s[0]
            e_ref[pl.ds(base, K), tok] = si_s[0].astype(jnp.int32)
            return carry

        lax.fori_loop(0, PEER_HEADS, head, 0)


def _route(x, pq, sk):
    T, D = x.shape
    tm = min(256, T)
    out = pl.BlockSpec((PEER_SLOTS, tm), lambda i: (0, i))
    n_cand = sum(PEER_TOPK // (a + 1) for a in range(PEER_TOPK))
    n_cand = -(-n_cand // 8) * 8
    return pl.pallas_call(
        _route_kernel,
        grid=(T // tm,),
        in_specs=[pl.BlockSpec((tm, D), lambda i: (i, 0)),
                  pl.BlockSpec((D, D), lambda i: (0, 0)),
                  pl.BlockSpec((2, PEER_KEYS, LANES), lambda i: (0, 0, 0))],
        out_specs=[out, out],
        out_shape=[jax.ShapeDtypeStruct((PEER_SLOTS, T), jnp.int32),
                   jax.ShapeDtypeStruct((PEER_SLOTS, T), F32)],
        scratch_shapes=[pltpu.VMEM((2 * PEER_HEADS, tm, LANES), BF16),
                        pltpu.VMEM((2, PEER_TOPK, LANES), F32),
                        pltpu.VMEM((2, PEER_TOPK, LANES), F32),
                        pltpu.VMEM((n_cand, LANES), F32),
                        pltpu.VMEM((n_cand, LANES), F32)],
        compiler_params=_cparams("parallel"),
        name="peer_route",
    )(x, pq, sk)


EXPERT_TOKENS_PER_STEP = 128
EXPERT_GROUP = 4
EXPERT_SETS = 4
EXPERT_AHEAD = 2


def _expert_kernel(idx_ref, nidx_ref, x_ref, gt_ref, lnw_ref, lnb_ref, uv_ref, o_ref, *scratch):
    G = EXPERT_GROUP
    nbuf = G * EXPERT_SETS
    bufs, sem = scratch[:nbuf], scratch[nbuf]
    TB = x_ref.shape[0]
    NE = PEER_SLOTS
    step = pl.program_id(0)
    per_chunk = NE // (2 * D_CHUNKS)

    def start(iref, t, e, b):
        pltpu.make_async_copy(uv_ref.at[iref[e, t]], bufs[b].at[:, e, :], sem.at[b]).start()

    def wait(b):
        pltpu.make_async_copy(bufs[b], bufs[b], sem.at[b]).wait()

    @pl.when(step == 0)
    def _():
        for b in range(EXPERT_AHEAD * G):
            def body(e, c, b=b):
                start(idx_ref, b, e, b)
                return c
            lax.fori_loop(0, NE, body, 0)

    lane = lax.broadcasted_iota(jnp.int32, (NE, TB), 1)

    def group(t0, s, issue):
        for k in range(G):
            wait(s * G + k)
        xrow = [x_ref[pl.ds(t0 + k, 1), :] for k in range(G)]

        def u_phase(k):
            buf = bufs[s * G + k]
            acc = jnp.zeros((NE, LANES), F32)
            for c in range(D_CHUNKS):
                for e in range(c * per_chunk, (c + 1) * per_chunk):
                    issue(k, e)
                u = lax.bitcast_convert_type(buf[c] & jnp.uint32(0xFFFF0000), F32)
                acc = acc + u * xrow[k][:, c * LANES:(c + 1) * LANES]
            h = jnp.sum(acc, axis=1, keepdims=True)
            gate = jnp.sum(jnp.where(lane == t0 + k, gt_ref[...], 0.0), axis=1, keepdims=True)
            return 0.5 * h * (1.0 + lax.erf(h * (2.0 ** -0.5))) * gate

        def v_phase(k, a):
            buf = bufs[s * G + k]
            ys = []
            for c in range(D_CHUNKS):
                for e in range((D_CHUNKS + c) * per_chunk, (D_CHUNKS + c + 1) * per_chunk):
                    issue(k, e)
                v = lax.bitcast_convert_type(buf[c] << 16, F32)
                ys.append(jnp.sum(v * a, axis=0, keepdims=True))
            z = DEEPNORM_ALPHA * xrow[k] + jnp.concatenate(ys, axis=1)
            o_ref[t0 + k] = _layer_norm(z, lnw_ref[...], lnb_ref[...])

        a = u_phase(0)
        for k in range(G):
            a_next = u_phase(k + 1) if k + 1 < G else None
            v_phase(k, a)
            a = a_next

    def ring(t0, last):
        for s in range(EXPERT_SETS):
            ns = (s + EXPERT_AHEAD) % EXPERT_SETS
            if last and s + EXPERT_AHEAD >= EXPERT_SETS:
                issue = lambda k, e, g0=(s + EXPERT_AHEAD - EXPERT_SETS) * G, ns=ns: start(
                    nidx_ref, g0 + k, e, ns * G + k)
            else:
                issue = lambda k, e, g0=t0 + (s + EXPERT_AHEAD) * G, ns=ns: start(idx_ref, g0 + k, e, ns * G + k)
            group(t0 + s * G, s, issue)

    def ring_body(q, c):
        ring(q * (G * EXPERT_SETS), False)
        return c

    lax.fori_loop(0, TB // (G * EXPERT_SETS) - 1, ring_body, 0)
    ring(TB - G * EXPERT_SETS, True)

    @pl.when(step == pl.num_programs(0) - 1)
    def _():
        for b in range(EXPERT_AHEAD * G):
            wait(b)


def _experts(exp_t, x, gates_t, lnw, lnb, uv):
    T, D = x.shape
    TB = min(EXPERT_TOKENS_PER_STEP, T)
    NE = PEER_SLOTS
    n = T // TB
    nbuf = EXPERT_GROUP * EXPERT_SETS
    out = pl.pallas_call(
        _expert_kernel,
        grid=(n,),
        in_specs=[pl.BlockSpec((NE, TB), lambda i: (0, i), memory_space=pltpu.SMEM),
                  pl.BlockSpec((NE, TB), lambda i: (0, jnp.minimum(i + 1, n - 1)), memory_space=pltpu.SMEM),
                  pl.BlockSpec((TB, D), lambda i: (i, 0)),
                  pl.BlockSpec((NE, TB), lambda i: (0, i)),
                  pl.BlockSpec((1, D), lambda i: (0, 0)),
                  pl.BlockSpec((1, D), lambda i: (0, 0)),
                  pl.BlockSpec(memory_space=pl.ANY)],
        out_specs=pl.BlockSpec((TB, 1, D), lambda i: (i, 0, 0)),
        out_shape=jax.ShapeDtypeStruct((T, 1, D), F32),
        scratch_shapes=[pltpu.VMEM((D_CHUNKS, NE, LANES), jnp.uint32)] * nbuf
        + [pltpu.SemaphoreType.DMA((nbuf,))],
        compiler_params=_cparams("arbitrary"),
        name="peer_experts",
    )(exp_t, exp_t, x, gates_t, lnw, lnb, uv)
    return out.reshape(T, D)


def kernel(x, mem, positions, w_in, shift_mu, w0, w_up, a0, a_up, g_up, k_k, k_a, r_k, lnx_w, lnx_b, lam_q1, lam_k1, lam_q2, lam_k2, subln_w, w_out, ln1_w, ln1_b, xq, xk, xv, xo, ln2_w, ln2_b, pq, subkeys, peer_u, peer_v, ln3_w, ln3_b):
    B, S, D = x.shape
    T = B * S
    M = mem.shape[1]
    H, N, C = RWKV_HEADS, RWKV_HEAD, RWKV_WIDTH
    xt = x.reshape(T, D)
    memt = mem.reshape(B * M, D)
    pos = positions.reshape(T, 1)
    row = lambda a: a.reshape(1, -1)
    lora_pad = LORA_PAD - LORA_COLS

    for l in range(DEPTH):
        wl = w_in[l]
        w_main = jnp.concatenate([wl[:, :3 * C], wl[:, RWKV_COLS:]], axis=1).astype(BF16)
        w_lora = jnp.pad(wl[:, 3 * C:RWKV_COLS], ((0, 0), (0, lora_pad))).astype(BF16)
        mu = shift_mu[l]
        mu_l = jnp.pad(mu[3 * C:], (0, lora_pad))
        o1, o2 = DECAY_LORA, DECAY_LORA + AAA_LORA
        wup = jnp.pad(w_up[l], ((0, LORA_PAD - o1), (0, 0))).astype(BF16)
        aup = jnp.pad(a_up[l], ((o1, LORA_PAD - o2), (0, 0))).astype(BF16)
        gup = jnp.pad(g_up[l], ((o2, lora_pad), (0, 0))).astype(BF16)

        main = _matmul(xt, w_main, F32, 512, 1024)
        lora = _matmul(xt, w_lora, F32, 512, LORA_PAD)
        r, lw, k2, v, kkr, ic, g = _rwkv_prep(
            main, lora, S, row(mu[:C]), row(mu[C:2 * C]), row(mu[2 * C:3 * C]), row(mu_l),
            row(w0[l]), row(a0[l]), row(k_k[l]), row(k_a[l]), wup, aup, gup)
        ya = _rwkv_scan(r, lw, k2, v, kkr, ic,
                        r_k[l].reshape(H, 1, N), lnx_w[l].reshape(H, 1, N), lnx_b[l].reshape(H, 1, N), S)

        qr, kr, vb = _rope(main, pos)
        yb = _diff_attention(qr, kr, vb, B, S, row(lam_q1[l]), row(lam_k1[l]), row(lam_q2[l]), row(lam_k2[l]),
                             row(subln_w[l]), l)

        wo = w_out[l].astype(BF16)
        x1 = _outproj(xt, ya, g, yb, wo[:C], wo[C:], row(ln1_w[l]), row(ln1_b[l]))

        kmem = _matmul(memt, xk[l].astype(BF16), BF16, 512, XATTN_WIDTH)
        vmem = _matmul(memt, xv[l].astype(BF16), BF16, 512, XATTN_WIDTH)
        x2 = _xattn(x1, S, xq[l].astype(BF16), kmem, vmem, xo[l].astype(BF16), row(ln2_w[l]), row(ln2_b[l]), M)

        exp_t, gates_t = _route(x2, pq[l].astype(BF16), subkeys[l].astype(BF16))
        half = lambda w: lax.bitcast_convert_type(w.astype(BF16), jnp.uint16).astype(jnp.uint32)
        uv = ((half(peer_u[l]) << 16) | half(peer_v[l])).reshape(PEER_EXPERTS, D_CHUNKS, LANES)
        xt = _experts(exp_t, x2, gates_t, row(ln3_w[l]), row(ln3_b[l]), uv)
    return xt.reshape(B, S, D)
```

```python
import functools
import math

import jax
import jax.numpy as jnp
from jax import lax
from jax.experimental import pallas as pl
from jax.experimental.pallas import tpu as pltpu

F32 = jnp.float32
BF16 = jnp.bfloat16

D_MODEL = 2048
DEPTH = 2
RWKV_WIDTH = 1024
RWKV_HEAD = 64
RWKV_HEADS = 16
DECAY_LORA = 64
AAA_LORA = 64
GATE_LORA = 160
LORA_COLS = DECAY_LORA + AAA_LORA + GATE_LORA
LORA_PAD = 384
RWKV_COLS = 3 * RWKV_WIDTH + LORA_COLS
DIFF_WIDTH = 1024
DIFF_VHEAD = 128
DIFF_HEADS = 8
DIFF_QK = 64
ROT_DIMS = 16
ROPE_THETA = 500000.0
XATTN_HEADS = 4
XATTN_HEAD = 128
XATTN_WIDTH = 512
PEER_HEADS = 8
PEER_KEYS = 128
PEER_EXPERTS = PEER_KEYS * PEER_KEYS
PEER_TOPK = 16
PEER_SLOTS = PEER_HEADS * PEER_TOPK
LN_EPS = 1e-5
GN_EPS = 64e-5
DEEPNORM_ALPHA = (2.0 * DEPTH) ** 0.25

LANES = 128
D_CHUNKS = D_MODEL // LANES
SCAN_CHUNK = 64
SCAN_HEADS_PER_STEP = 8
SCAN_CHUNKS_PER_STEP = 2
VMEM_LIMIT = 48 * 1024 * 1024

_NT = (((1,), (1,)), ((), ()))
_TN = (((0,), (0,)), ((), ()))


def _cparams(*sem):
    return pltpu.CompilerParams(dimension_semantics=sem, vmem_limit_bytes=VMEM_LIMIT)


def _layer_norm(z, w, b):
    mu = jnp.mean(z, axis=-1, keepdims=True)
    zc = z - mu
    var = jnp.mean(zc * zc, axis=-1, keepdims=True)
    return zc * lax.rsqrt(var + LN_EPS) * w + b


def _mm_kernel(a_ref, w_ref, o_ref):
    o_ref[...] = jnp.dot(a_ref[...].astype(BF16), w_ref[...],
                         preferred_element_type=F32).astype(o_ref.dtype)


def _matmul(a, w, out_dtype, tm, tn):
    M, K = a.shape
    N = w.shape[1]
    tm, tn = min(tm, M), min(tn, N)
    return pl.pallas_call(
        _mm_kernel,
        grid=(N // tn, M // tm),
        in_specs=[pl.BlockSpec((tm, K), lambda j, i: (i, 0)),
                  pl.BlockSpec((K, tn), lambda j, i: (0, j))],
        out_specs=pl.BlockSpec((tm, tn), lambda j, i: (i, j)),
        out_shape=jax.ShapeDtypeStruct((M, N), out_dtype),
        compiler_params=_cparams("parallel", "parallel"),
        name="matmul",
    )(a, w)


def _rwkv_prep_kernel(r_ref, k_ref, v_ref, l_ref, rp_ref, kp_ref, vp_ref, lp_ref,
                      mur_ref, muk_ref, muv_ref, mul_ref, w0_ref, a0_ref, kk_ref, ka_ref,
                      wup_ref, aup_ref, gup_ref,
                      ro_ref, lw_ref, k2_ref, vo_ref, kkr_ref, ic_ref, g_ref, *, tiles_per_seq):
    i = pl.program_id(0)
    first = (i % tiles_per_seq) == 0

    def shifted(cur_ref, prev_ref, mu_ref):
        p = cur_ref[...]
        tm = p.shape[0]
        prev_row = jnp.where(first, 0.0, prev_ref[7:8, :])
        row = lax.broadcasted_iota(jnp.int32, p.shape, 0)
        p_prev = jnp.where(row == 0, prev_row, pltpu.roll(p, 1, 0))
        del tm
        return p + (p_prev - p) * mu_ref[...]

    r = shifted(r_ref, rp_ref, mur_ref)
    k = shifted(k_ref, kp_ref, muk_ref)
    v = shifted(v_ref, vp_ref, muv_ref)
    lo = shifted(l_ref, lp_ref, mul_ref)

    wl = w0_ref[...] + jnp.dot(jnp.tanh(lo).astype(BF16), wup_ref[...], preferred_element_type=F32)
    nz = -wl
    softplus = jnp.maximum(nz, 0.0) + jnp.log(1.0 + jnp.exp(-jnp.abs(nz)))
    w_log = -softplus - 0.5
    lw_ref[...] = -jnp.exp(w_log)
    al = a0_ref[...] + jnp.dot(lo.astype(BF16), aup_ref[...], preferred_element_type=F32)
    iclr = 1.0 / (1.0 + jnp.exp(-al))
    sg = 1.0 / (1.0 + jnp.exp(-lo))
    g_ref[...] = jnp.dot(sg.astype(BF16), gup_ref[...], preferred_element_type=F32)
    ro_ref[...] = r
    vo_ref[...] = v
    kkr_ref[...] = k * kk_ref[...]
    ic_ref[...] = iclr
    k2_ref[...] = k * (1.0 + (iclr - 1.0) * ka_ref[...])


def _rwkv_prep(main, lora, S, mu_r, mu_k, mu_v, mu_l, w0, a0, k_k, k_a, wup, aup, gup):
    T = main.shape[0]
    tm = min(256, S)
    C = RWKV_WIDTH
    prev = lambda c: (lambda i: (jnp.maximum(i * (tm // 8) - 1, 0), c))
    cur = lambda c: (lambda i: (i, c))
    vec = lambda n: pl.BlockSpec((1, n), lambda i: (0, 0))
    mat = pl.BlockSpec((LORA_PAD, C), lambda i: (0, 0))
    out = pl.BlockSpec((tm, C), lambda i: (i, 0))
    return pl.pallas_call(
        functools.partial(_rwkv_prep_kernel, tiles_per_seq=S // tm),
        grid=(T // tm,),
        in_specs=[pl.BlockSpec((tm, C), cur(0)), pl.BlockSpec((tm, C), cur(1)), pl.BlockSpec((tm, C), cur(2)),
                  pl.BlockSpec((tm, LORA_PAD), cur(0)),
                  pl.BlockSpec((8, C), prev(0)), pl.BlockSpec((8, C), prev(1)), pl.BlockSpec((8, C), prev(2)),
                  pl.BlockSpec((8, LORA_PAD), prev(0)),
                  vec(C), vec(C), vec(C), vec(LORA_PAD), vec(C), vec(C), vec(C), vec(C),
                  mat, mat, mat],
        out_specs=[out] * 7,
        out_shape=[jax.ShapeDtypeStruct((T, C), F32)] * 7,
        compiler_params=_cparams("parallel"),
        name="rwkv_prep",
    )(main, main, main, lora, main, main, main, lora,
      mu_r, mu_k, mu_v, mu_l, w0, a0, k_k, k_a, wup, aup, gup)


def _rwkv_scan_kernel(r_ref, lw_ref, k_ref, v_ref, kkr_ref, ic_ref, rk_ref, lnw_ref, lnb_ref,
                      y_ref, state):
    L = SCAN_CHUNK
    N = RWKV_HEAD
    win = lambda ref, q: ref[q[1] * L:(q[1] + 1) * L, q[0] * N:(q[0] + 1) * N]

    @pl.when(pl.program_id(1) == 0)
    def _():
        state[...] = jnp.zeros_like(state)

    row = lax.broadcasted_iota(jnp.int32, (L, L), 0)
    col = lax.broadcasted_iota(jnp.int32, (L, L), 1)
    tril_incl = (col <= row).astype(F32)
    eye = (col == row).astype(F32)
    row2 = lax.broadcasted_iota(jnp.int32, (L, 2 * L), 0)
    col2 = lax.broadcasted_iota(jnp.int32, (L, 2 * L), 1) % L
    zeros_ln = jnp.zeros((L, N), BF16)
    bdot = lambda a, b: jnp.dot(a.astype(BF16), b.astype(BF16), preferred_element_type=F32)

    heads = SCAN_HEADS_PER_STEP
    pairs = [(hd, c) for c in range(SCAN_CHUNKS_PER_STEP) for hd in range(heads)]
    G = range(len(pairs))
    r = [win(r_ref, q) for q in pairs]
    lw = [win(lw_ref, q) for q in pairs]
    k = [win(k_ref, q) for q in pairs]
    v = [win(v_ref, q) for q in pairs]
    lp = [jnp.dot(tril_incl, lw[g], precision=lax.Precision.HIGHEST, preferred_element_type=F32) for g in G]
    ep = [jnp.exp(lp[g]) for g in G]
    en = [jnp.exp(-lp[g]) for g in G]
    kk = []
    for q in pairs:
        kkr = win(kkr_ref, q)
        nrm = jnp.sqrt(jnp.sum(kkr * kkr, axis=-1, keepdims=True))
        kk.append(kkr / jnp.maximum(nrm, 1e-12))
    at = [((-kk[g]) * jnp.exp(lp[g] - lw[g])).astype(BF16) for g in G]
    rt = [r[g] * ep[g] for g in G]
    bt = [((kk[g] * win(ic_ref, pairs[g])) * en[g]).astype(BF16) for g in G]
    bk = [jnp.concatenate([bt[g], (k[g] * en[g]).astype(BF16)], axis=0) for g in G]
    vb = [v[g].astype(BF16) for g in G]
    m_a = [jnp.where(col2 < row2, lax.dot_general(at[g], bk[g], _NT, preferred_element_type=F32), 0.0) for g in G]
    m_r = [jnp.where(col2 <= row2, lax.dot_general(rt[g].astype(BF16), bk[g], _NT, preferred_element_type=F32),
                     0.0).astype(BF16) for g in G]
    x = [jnp.dot(m_a[g].astype(BF16), jnp.concatenate([zeros_ln, vb[g]], axis=0), preferred_element_type=F32)
         for g in G]
    square = lambda m: [bdot(m[g], m[g]) for g in G]
    pair = lambda lo, hi: [(eye + lo[g]) + bdot(eye + lo[g], hi[g]) for g in G]
    p1 = [m_a[g][:, :L] for g in G]
    p2 = square(p1)
    f01 = pair(p1, p2)
    p4 = square(p2)
    p8 = square(p4)
    f23 = pair(p4, p8)
    f03 = [bdot(f01[g], f23[g]) for g in G]
    p16 = square(p8)
    p32 = square(p16)
    f45 = pair(p16, p32)
    tb = [bdot(f03[g], f45[g]).astype(BF16) for g in G]
    w_b = [jnp.dot(tb[g], at[g], preferred_element_type=F32).astype(BF16) for g in G]
    u0_b = [jnp.dot(tb[g], x[g].astype(BF16), preferred_element_type=F32).astype(BF16) for g in G]
    uv = [jnp.concatenate([u0_b[g], vb[g]], axis=0) for g in G]
    rp = [rt[g] + jnp.dot(m_r[g], jnp.concatenate([w_b[g], zeros_ln], axis=0), preferred_element_type=F32)
          for g in G]
    y0 = [jnp.dot(m_r[g], uv[g], preferred_element_type=F32) for g in G]
    p_last = [ep[g][L - 1:L, :] for g in G]
    g_mat = [(eye + lax.dot_general(w_b[g], bt[g], _TN, preferred_element_type=F32)) * p_last[g] for g in G]
    h_mat = [lax.dot_general(uv[g], bk[g], _TN, preferred_element_type=F32) * p_last[g] for g in G]
    st = [state[hd] for hd in range(heads)]
    y = [None] * len(pairs)
    for g in G:
        hd = pairs[g][0]
        y[g] = lax.dot_general(rp[g].astype(BF16), st[hd].astype(BF16), _NT, preferred_element_type=F32) + y0[g]
        st[hd] = bdot(st[hd], g_mat[g]) + h_mat[g]
    for hd in range(heads):
        state[hd] = st[hd]
    for g in G:
        hd, c = pairs[g]
        mu = jnp.mean(y[g], axis=-1, keepdims=True)
        yc = y[g] - mu
        var = jnp.mean(yc * yc, axis=-1, keepdims=True)
        yn = yc * lax.rsqrt(var + GN_EPS) * lnw_ref[hd] + lnb_ref[hd]
        bonus = jnp.sum(r[g] * k[g] * rk_ref[hd], axis=-1, keepdims=True) * v[g]
        y_ref[c * L:(c + 1) * L, hd * N:(hd + 1) * N] = yn + bonus


def _rwkv_scan(r, lw, k2, v, kkr, ic, rk, lnw, lnb, S):
    T = r.shape[0]
    N = RWKV_HEAD
    G = SCAN_HEADS_PER_STEP
    rows = SCAN_CHUNK * SCAN_CHUNKS_PER_STEP
    groups_per_batch = RWKV_HEADS // G
    steps = S // rows
    blk = pl.BlockSpec((rows, G * N), lambda i, c: (i // groups_per_batch * steps + c, i % groups_per_batch))
    par = pl.BlockSpec((G, 1, N), lambda i, c: (i % groups_per_batch, 0, 0))
    return pl.pallas_call(
        _rwkv_scan_kernel,
        grid=(T // S * groups_per_batch, steps),
        in_specs=[blk] * 6 + [par] * 3,
        out_specs=blk,
        out_shape=jax.ShapeDtypeStruct((T, RWKV_WIDTH), F32),
        scratch_shapes=[pltpu.VMEM((G, N, N), F32)],
        compiler_params=_cparams("parallel", "arbitrary"),
        name="rwkv_scan",
    )(r, lw, k2, v, kkr, ic, rk, lnw, lnb)


def _rope_kernel(q_ref, k_ref, v_ref, pos_ref, qo_ref, ko_ref, vo_ref):
    tm = q_ref.shape[0]
    lane = lax.broadcasted_iota(jnp.int32, (1, LANES), 1)
    d = lane % DIFF_QK
    half = ROT_DIMS // 2
    freq = jnp.exp((d % half).astype(F32) * (-2.0 * math.log(ROPE_THETA) / ROT_DIMS))
    freq = jnp.where(d < ROT_DIMS, freq, 0.0)
    ang = pos_ref[...].astype(F32) * freq
    cos = jnp.cos(ang)
    sin = jnp.sin(ang)
    c_mul = jnp.where(d < ROT_DIMS, cos, 1.0)
    s_lo = jnp.where(d < half, -sin, 0.0)
    s_hi = jnp.where((d >= half) & (d < ROT_DIMS), sin, 0.0)
    del tm

    def rot(t):
        return t * c_mul + pltpu.roll(t, LANES - half, 1) * s_lo + pltpu.roll(t, half, 1) * s_hi

    scale = DIFF_QK ** -0.5
    for h in range(DIFF_HEADS):
        sl = slice(h * LANES, (h + 1) * LANES)
        qo_ref[:, sl] = (rot(q_ref[:, sl]) * scale).astype(BF16)
        ko_ref[:, sl] = rot(k_ref[:, sl]).astype(BF16)
    vo_ref[...] = v_ref[...].astype(BF16)


def _rope(main, pos):
    T = main.shape[0]
    tm = min(256, T)
    C = DIFF_WIDTH
    base = 3 * RWKV_WIDTH // C
    blk = lambda c: pl.BlockSpec((tm, C), lambda i: (i, base + c))
    out = pl.BlockSpec((tm, C), lambda i: (i, 0))
    return pl.pallas_call(
        _rope_kernel,
        grid=(T // tm,),
        in_specs=[blk(0), blk(1), blk(2), pl.BlockSpec((tm, 1), lambda i: (i, 0))],
        out_specs=[out] * 3,
        out_shape=[jax.ShapeDtypeStruct((T, C), BF16)] * 3,
        compiler_params=_cparams("parallel"),
        name="rope",
    )(main, main, main, pos)


def _flash_kernel(q_ref, k_ref, v_ref, lq1_ref, lk1_ref, lq2_ref, lk2_ref, sub_ref, o_ref, *, lam_init, tq):
    i = pl.program_id(2)
    tk = tq
    q = q_ref[...]
    lane = lax.broadcasted_iota(jnp.int32, q.shape, 1)
    zero = jnp.zeros_like(q)
    qq = jnp.concatenate([jnp.where(lane < DIFF_QK, q, zero), jnp.where(lane >= DIFF_QK, q, zero)], axis=0)

    def block(j, carry, diagonal):
        m_prev, l_prev, acc = carry
        off = pl.multiple_of(j * tk, tk)
        k = k_ref[pl.ds(off, tk), :]
        v = v_ref[pl.ds(off, tk), :]
        s = lax.dot_general(qq, k, _NT, preferred_element_type=F32)
        if diagonal:
            row = lax.broadcasted_iota(jnp.int32, (2 * tq, tk), 0)
            col = lax.broadcasted_iota(jnp.int32, (2 * tq, tk), 1)
            s = jnp.where(col <= jnp.where(row >= tq, row - tq, row), s, -jnp.inf)
        m_new = jnp.maximum(m_prev, jnp.max(s, axis=1, keepdims=True))
        a = jnp.exp(m_prev - m_new)
        p = jnp.exp(s - m_new)
        l_new = a * l_prev + jnp.sum(p, axis=1, keepdims=True)
        acc = a * acc + jnp.dot(p.astype(BF16), v, preferred_element_type=F32)
        return m_new, l_new, acc

    init = (jnp.full((2 * tq, 1), -jnp.inf, F32), jnp.zeros((2 * tq, 1), F32), jnp.zeros((2 * tq, LANES), F32))
    carry = lax.fori_loop(0, i, lambda j, c: block(j, c, False), init)
    _, l_fin, acc = block(i, carry, True)
    lam = (jnp.exp(jnp.sum(lq1_ref[...] * lk1_ref[...], keepdims=True))
           - jnp.exp(jnp.sum(lq2_ref[...] * lk2_ref[...], keepdims=True)) + lam_init)
    on = acc / l_fin
    o = on[:tq] - lam * on[tq:]
    o = o * lax.rsqrt(jnp.mean(o * o, axis=-1, keepdims=True) + LN_EPS) * sub_ref[...]
    o_ref[...] = o * (1.0 - lam_init)


def _diff_attention(q, k, v, B, S, lq1, lk1, lq2, lk2, subw, layer_idx):
    T = q.shape[0]
    tq = min(1024, S)
    nq = S // tq
    lam_init = 0.8 - 0.6 * math.exp(-0.3 * layer_idx)
    vec = pl.BlockSpec((1, DIFF_QK), lambda b, h, i: (0, 0))
    seq = pl.BlockSpec((S, LANES), lambda b, h, i: (b, h))
    return pl.pallas_call(
        functools.partial(_flash_kernel, lam_init=lam_init, tq=tq),
        grid=(B, DIFF_HEADS, nq),
        in_specs=[pl.BlockSpec((tq, LANES), lambda b, h, i: (b * nq + i, h)), seq, seq,
                  vec, vec, vec, vec,
                  pl.BlockSpec((1, DIFF_VHEAD), lambda b, h, i: (0, 0))],
        out_specs=pl.BlockSpec((tq, LANES), lambda b, h, i: (b * nq + i, h)),
        out_shape=jax.ShapeDtypeStruct((T, DIFF_WIDTH), F32),
        compiler_params=_cparams("parallel", "parallel", "parallel"),
        name="diff_attention",
    )(q, k, v, lq1, lk1, lq2, lk2, subw)


def _outproj_kernel(x_ref, ya_ref, g_ref, yb_ref, wa_ref, wb_ref, lnw_ref, lnb_ref, o_ref):
    ya = (ya_ref[...] * g_ref[...]).astype(BF16)
    mix = (jnp.dot(ya, wa_ref[...], preferred_element_type=F32)
           + jnp.dot(yb_ref[...].astype(BF16), wb_ref[...], preferred_element_type=F32))
    o_ref[...] = _layer_norm(DEEPNORM_ALPHA * x_ref[...] + mix, lnw_ref[...], lnb_ref[...])


def _outproj(x, ya, g, yb, wa, wb, lnw, lnb):
    T, D = x.shape
    tm = min(256, T)
    half = pl.BlockSpec((tm, RWKV_WIDTH), lambda i: (i, 0))
    full = pl.BlockSpec((tm, D), lambda i: (i, 0))
    wsp = pl.BlockSpec((RWKV_WIDTH, D), lambda i: (0, 0))
    vec = pl.BlockSpec((1, D), lambda i: (0, 0))
    return pl.pallas_call(
        _outproj_kernel,
        grid=(T // tm,),
        in_specs=[full, half, half, half, wsp, wsp, vec, vec],
        out_specs=full,
        out_shape=jax.ShapeDtypeStruct((T, D), F32),
        compiler_params=_cparams("parallel"),
        name="outproj_ln",
    )(x, ya, g, yb, wa, wb, lnw, lnb)


def _xattn_kernel(x_ref, wq_ref, k_ref, v_ref, wo_ref, lnw_ref, lnb_ref, o_ref):
    x = x_ref[...]
    q = jnp.dot(x.astype(BF16), wq_ref[...], preferred_element_type=F32) * (XATTN_HEAD ** -0.5)
    outs = []
    for h in range(XATTN_HEADS):
        sl = slice(h * XATTN_HEAD, (h + 1) * XATTN_HEAD)
        s = lax.dot_general(q[:, sl].astype(BF16), k_ref[:, sl], _NT, preferred_element_type=F32)
        p = jnp.exp(s - jnp.max(s, axis=-1, keepdims=True))
        p = p / jnp.sum(p, axis=-1, keepdims=True)
        outs.append(jnp.dot(p.astype(BF16), v_ref[:, sl], preferred_element_type=F32))
    o = jnp.concatenate(outs, axis=1).astype(BF16)
    xa = jnp.dot(o, wo_ref[...], preferred_element_type=F32)
    o_ref[...] = _layer_norm(DEEPNORM_ALPHA * x + xa, lnw_ref[...], lnb_ref[...])


def _xattn(x, S, wq, kmem, vmem, wo, lnw, lnb, mem_len):
    T, D = x.shape
    tm = min(256, S)
    tiles_per_seq = S // tm
    full = pl.BlockSpec((tm, D), lambda i: (i, 0))
    kv = pl.BlockSpec((mem_len, XATTN_WIDTH), lambda i: (i // tiles_per_seq, 0))
    vec = pl.BlockSpec((1, D), lambda i: (0, 0))
    return pl.pallas_call(
        _xattn_kernel,
        grid=(T // tm,),
        in_specs=[full, pl.BlockSpec((D, XATTN_WIDTH), lambda i: (0, 0)), kv, kv,
                  pl.BlockSpec((XATTN_WIDTH, D), lambda i: (0, 0)), vec, vec],
        out_specs=full,
        out_shape=jax.ShapeDtypeStruct((T, D), F32),
        compiler_params=_cparams("parallel"),
        name="xattn_ln",
    )(x, wq, kmem, vmem, wo, lnw, lnb)


ROUTE_HEADS_PER_ITER = 4


def _route_kernel(x_ref, pq_ref, sk_ref, e_ref, g_ref, q_s, sv_s, si_s, cand_s, eid_s):
    tm = x_ref.shape[0]
    K = PEER_TOPK
    NK = PEER_KEYS
    q = jnp.dot(x_ref[...].astype(BF16), pq_ref[...], preferred_element_type=F32)
    for hc in range(2 * PEER_HEADS):
        q_s[hc] = q[:, hc * LANES:(hc + 1) * LANES].astype(BF16)
    neg = -jnp.inf
    n_b = [K // (a + 1) for a in range(K)]
    offs = [sum(n_b[:a]) for a in range(K)]
    n_cand = cand_s.shape[1]
    P = ROUTE_HEADS_PER_ITER

    for sub in range(tm // LANES):
        tok = slice(sub * LANES, (sub + 1) * LANES)
        rowk = lax.broadcasted_iota(jnp.int32, (NK, LANES), 0).astype(F32)
        rowc = lax.broadcasted_iota(jnp.int32, (n_cand, LANES), 0).astype(F32)

        def heads(hp, carry):
            for p in range(P):
                s = [lax.dot_general(sk_ref[c], q_s[2 * (hp * P + p) + c, tok, :], _NT,
                                     preferred_element_type=F32) for c in range(2)]
                for j in range(K):
                    for c in range(2):
                        m = jnp.max(s[c], axis=0, keepdims=True)
                        idx = jnp.min(jnp.where(s[c] == m, rowk, float(NK)), axis=0, keepdims=True)
                        s[c] = jnp.where(rowk == idx, neg, s[c])
                        sv_s[p, c, j:j + 1, :] = m
                        si_s[p, c, j:j + 1, :] = idx
                cand_s[p, n_cand - 8:, :] = jnp.full((8, LANES), neg, F32)
                eid_s[p, n_cand - 8:, :] = jnp.zeros((8, LANES), F32)
                for a in range(K):
                    rows = slice(offs[a], offs[a] + n_b[a])
                    cand_s[p, rows, :] = sv_s[p, 0, a:a + 1, :] + sv_s[p, 1, 0:n_b[a], :]
                    eid_s[p, rows, :] = si_s[p, 0, a:a + 1, :] * float(NK) + si_s[p, 1, 0:n_b[a], :]
            cand = [cand_s[p] for p in range(P)]
            eid = [eid_s[p] for p in range(P)]
            fv = [[] for _ in range(P)]
            fe = [[] for _ in range(P)]
            for j in range(K):
                for p in range(P):
                    m = jnp.max(cand[p], axis=0, keepdims=True)
                    idx = jnp.min(jnp.where(cand[p] == m, rowc, float(n_cand)), axis=0, keepdims=True)
                    sel = rowc == idx
                    fe[p].append(jnp.max(jnp.where(sel, eid[p], -1.0), axis=0, keepdims=True))
                    cand[p] = jnp.where(sel, neg, cand[p])
                    fv[p].append(m)
            for p in range(P):
                w = [jnp.exp(fv[p][j] - fv[p][0]) for j in range(K)]
                den = w[0]
                for j in range(1, K):
                    den = den + w[j]
                for j in range(K):
                    sv_s[p, 0, j:j + 1, :] = w[j] / den
                    si_s[p, 0, j:j + 1, :] = fe[p][j]
                base = pl.multiple_of((hp * P + p) * K, K)
                g_ref[pl.ds(base, K), tok] = sv_s[p, 0]
                e_ref[pl.ds(base, K), tok] = si_s[p, 0].astype(jnp.int32)
            return carry

        lax.fori_loop(0, PEER_HEADS // P, heads, 0)


def _route(x, pq, sk):
    T, D = x.shape
    tm = min(256, T)
    out = pl.BlockSpec((PEER_SLOTS, tm), lambda i: (0, i))
    n_cand = sum(PEER_TOPK // (a + 1) for a in range(PEER_TOPK))
    n_cand = -(-n_cand // 8) * 8
    return pl.pallas_call(
        _route_kernel,
        grid=(T // tm,),
        in_specs=[pl.BlockSpec((tm, D), lambda i: (i, 0)),
                  pl.BlockSpec((D, D), lambda i: (0, 0)),
                  pl.BlockSpec((2, PEER_KEYS, LANES), lambda i: (0, 0, 0))],
        out_specs=[out, out],
        out_shape=[jax.ShapeDtypeStruct((PEER_SLOTS, T), jnp.int32),
                   jax.ShapeDtypeStruct((PEER_SLOTS, T), F32)],
        scratch_shapes=[pltpu.VMEM((2 * PEER_HEADS, tm, LANES), BF16),
                        pltpu.VMEM((ROUTE_HEADS_PER_ITER, 2, PEER_TOPK, LANES), F32),
                        pltpu.VMEM((ROUTE_HEADS_PER_ITER, 2, PEER_TOPK, LANES), F32),
                        pltpu.VMEM((ROUTE_HEADS_PER_ITER, n_cand, LANES), F32),
                        pltpu.VMEM((ROUTE_HEADS_PER_ITER, n_cand, LANES), F32)],
        compiler_params=_cparams("parallel"),
        name="peer_route",
    )(x, pq, sk)


EXPERT_TOKENS_PER_STEP = 128
EXPERT_GROUP = 4
EXPERT_SETS = 4
EXPERT_AHEAD = 2


def _expert_kernel(idx_ref, nidx_ref, x_ref, gt_ref, lnw_ref, lnb_ref, uv_ref, o_ref, *scratch):
    G = EXPERT_GROUP
    nbuf = G * EXPERT_SETS
    bufs, sem = scratch[:nbuf], scratch[nbuf]
    TB = x_ref.shape[0]
    NE = PEER_SLOTS
    step = pl.program_id(0)
    per_chunk = NE // (2 * D_CHUNKS)

    def start(iref, t, e, b):
        pltpu.make_async_copy(uv_ref.at[iref[e, t]], bufs[b].at[:, e, :], sem.at[b]).start()

    def wait(b):
        pltpu.make_async_copy(bufs[b], bufs[b], sem.at[b]).wait()

    @pl.when(step == 0)
    def _():
        for b in range(EXPERT_AHEAD * G):
            def body(e, c, b=b):
                start(idx_ref, b, e, b)
                return c
            lax.fori_loop(0, NE, body, 0)

    lane = lax.broadcasted_iota(jnp.int32, (NE, TB), 1)

    def group(t0, s, issue):
        for k in range(G):
            wait(s * G + k)
        xrow = [x_ref[pl.ds(t0 + k, 1), :] for k in range(G)]

        def u_phase(k):
            buf = bufs[s * G + k]
            acc = jnp.zeros((NE, LANES), F32)
            for c in range(D_CHUNKS):
                for e in range(c * per_chunk, (c + 1) * per_chunk):
                    issue(k, e)
                u = lax.bitcast_convert_type(buf[c] & jnp.uint32(0xFFFF0000), F32)
                acc = acc + u * xrow[k][:, c * LANES:(c + 1) * LANES]
            h = jnp.sum(acc, axis=1, keepdims=True)
            gate = jnp.sum(jnp.where(lane == t0 + k, gt_ref[...], 0.0), axis=1, keepdims=True)
            return 0.5 * h * (1.0 + lax.erf(h * (2.0 ** -0.5))) * gate

        def v_phase(k, a):
            buf = bufs[s * G + k]
            ys = []
            for c in range(D_CHUNKS):
                for e in range((D_CHUNKS + c) * per_chunk, (D_CHUNKS + c + 1) * per_chunk):
                    issue(k, e)
                v = lax.bitcast_convert_type(buf[c] << 16, F32)
                ys.append(jnp.sum(v * a, axis=0, keepdims=True))
            z = DEEPNORM_ALPHA * xrow[k] + jnp.concatenate(ys, axis=1)
            o_ref[t0 + k] = _layer_norm(z, lnw_ref[...], lnb_ref[...])

        a = u_phase(0)
        for k in range(G):
            a_next = u_phase(k + 1) if k + 1 < G else None
            v_phase(k, a)
            a = a_next

    def ring(t0, last):
        for s in range(EXPERT_SETS):
            ns = (s + EXPERT_AHEAD) % EXPERT_SETS
            if last and s + EXPERT_AHEAD >= EXPERT_SETS:
                issue = lambda k, e, g0=(s + EXPERT_AHEAD - EXPERT_SETS) * G, ns=ns: start(
                    nidx_ref, g0 + k, e, ns * G + k)
            else:
                issue = lambda k, e, g0=t0 + (s + EXPERT_AHEAD) * G, ns=ns: start(idx_ref, g0 + k, e, ns * G + k)
            group(t0 + s * G, s, issue)

    def ring_body(q, c):
        ring(q * (G * EXPERT_SETS), False)
        return c

    lax.fori_loop(0, TB // (G * EXPERT_SETS) - 1, ring_body, 0)
    ring(TB - G * EXPERT_SETS, True)

    @pl.when(step == pl.num_programs(0) - 1)
    def _():
        for b in range(EXPERT_AHEAD * G):
            wait(b)


def _experts(exp_t, x, gates_t, lnw, lnb, uv):
    T, D = x.shape
    TB = min(EXPERT_TOKENS_PER_STEP, T)
    NE = PEER_SLOTS
    n = T // TB
    nbuf = EXPERT_GROUP * EXPERT_SETS
    out = pl.pallas_call(
        _expert_kernel,
        grid=(n,),
        in_specs=[pl.BlockSpec((NE, TB), lambda i: (0, i), memory_space=pltpu.SMEM),
                  pl.BlockSpec((NE, TB), lambda i: (0, jnp.minimum(i + 1, n - 1)), memory_space=pltpu.SMEM),
                  pl.BlockSpec((TB, D), lambda i: (i, 0)),
                  pl.BlockSpec((NE, TB), lambda i: (0, i)),
                  pl.BlockSpec((1, D), lambda i: (0, 0)),
                  pl.BlockSpec((1, D), lambda i: (0, 0)),
                  pl.BlockSpec(memory_space=pl.ANY)],
        out_specs=pl.BlockSpec((TB, 1, D), lambda i: (i, 0, 0)),
        out_shape=jax.ShapeDtypeStruct((T, 1, D), F32),
        scratch_shapes=[pltpu.VMEM((D_CHUNKS, NE, LANES), jnp.uint32)] * nbuf
        + [pltpu.SemaphoreType.DMA((nbuf,))],
        compiler_params=_cparams("arbitrary"),
        name="peer_experts",
    )(exp_t, exp_t, x, gates_t, lnw, lnb, uv)
    return out.reshape(T, D)


def kernel(x, mem, positions, w_in, shift_mu, w0, w_up, a0, a_up, g_up, k_k, k_a, r_k, lnx_w, lnx_b, lam_q1, lam_k1, lam_q2, lam_k2, subln_w, w_out, ln1_w, ln1_b, xq, xk, xv, xo, ln2_w, ln2_b, pq, subkeys, peer_u, peer_v, ln3_w, ln3_b):
    B, S, D = x.shape
    T = B * S
    M = mem.shape[1]
    H, N, C = RWKV_HEADS, RWKV_HEAD, RWKV_WIDTH
    xt = x.reshape(T, D)
    memt = mem.reshape(B * M, D)
    pos = positions.reshape(T, 1)
    row = lambda a: a.reshape(1, -1)
    lora_pad = LORA_PAD - LORA_COLS

    for l in range(DEPTH):
        wl = w_in[l]
        w_main = jnp.concatenate([wl[:, :3 * C], wl[:, RWKV_COLS:]], axis=1).astype(BF16)
        w_lora = jnp.pad(wl[:, 3 * C:RWKV_COLS], ((0, 0), (0, lora_pad))).astype(BF16)
        mu = shift_mu[l]
        mu_l = jnp.pad(mu[3 * C:], (0, lora_pad))
        o1, o2 = DECAY_LORA, DECAY_LORA + AAA_LORA
        wup = jnp.pad(w_up[l], ((0, LORA_PAD - o1), (0, 0))).astype(BF16)
        aup = jnp.pad(a_up[l], ((o1, LORA_PAD - o2), (0, 0))).astype(BF16)
        gup = jnp.pad(g_up[l], ((o2, lora_pad), (0, 0))).astype(BF16)

        main = _matmul(xt, w_main, F32, 512, 1024)
        lora = _matmul(xt, w_lora, F32, 512, LORA_PAD)
        r, lw, k2, v, kkr, ic, g = _rwkv_prep(
            main, lora, S, row(mu[:C]), row(mu[C:2 * C]), row(mu[2 * C:3 * C]), row(mu_l),
            row(w0[l]), row(a0[l]), row(k_k[l]), row(k_a[l]), wup, aup, gup)
        ya = _rwkv_scan(r, lw, k2, v, kkr, ic,
                        r_k[l].reshape(H, 1, N), lnx_w[l].reshape(H, 1, N), lnx_b[l].reshape(H, 1, N), S)

        qr, kr, vb = _rope(main, pos)
        yb = _diff_attention(qr, kr, vb, B, S, row(lam_q1[l]), row(lam_k1[l]), row(lam_q2[l]), row(lam_k2[l]),
                             row(subln_w[l]), l)

        wo = w_out[l].astype(BF16)
        x1 = _outproj(xt, ya, g, yb, wo[:C], wo[C:], row(ln1_w[l]), row(ln1_b[l]))

        kmem = _matmul(memt, xk[l].astype(BF16), BF16, 512, XATTN_WIDTH)
        vmem = _matmul(memt, xv[l].astype(BF16), BF16, 512, XATTN_WIDTH)
        x2 = _xattn(x1, S, xq[l].astype(BF16), kmem, vmem, xo[l].astype(BF16), row(ln2_w[l]), row(ln2_b[l]), M)

        exp_t, gates_t = _route(x2, pq[l].astype(BF16), subkeys[l].astype(BF16))
        half = lambda w: lax.bitcast_convert_type(w.astype(BF16), jnp.uint16).astype(jnp.uint32)
        uv = ((half(peer_u[l]) << 16) | half(peer_v[l])).reshape(PEER_EXPERTS, D_CHUNKS, LANES)
        xt = _experts(exp_t, x2, gates_t, row(ln3_w[l]), row(ln3_b[l]), uv)
    return xt.reshape(B, S, D)
```

```python
import functools
import math

import jax
import jax.numpy as jnp
from jax import lax
from jax.experimental import pallas as pl
from jax.experimental.pallas import tpu as pltpu

F32 = jnp.float32
BF16 = jnp.bfloat16

D_MODEL = 2048
DEPTH = 2
RWKV_WIDTH = 1024
RWKV_HEAD = 64
RWKV_HEADS = 16
DECAY_LORA = 64
AAA_LORA = 64
GATE_LORA = 160
LORA_COLS = DECAY_LORA + AAA_LORA + GATE_LORA
LORA_PAD = 384
RWKV_COLS = 3 * RWKV_WIDTH + LORA_COLS
DIFF_WIDTH = 1024
DIFF_VHEAD = 128
DIFF_HEADS = 8
DIFF_QK = 64
ROT_DIMS = 16
ROPE_THETA = 500000.0
XATTN_HEADS = 4
XATTN_HEAD = 128
XATTN_WIDTH = 512
PEER_HEADS = 8
PEER_KEYS = 128
PEER_EXPERTS = PEER_KEYS * PEER_KEYS
PEER_TOPK = 16
PEER_SLOTS = PEER_HEADS * PEER_TOPK
LN_EPS = 1e-5
GN_EPS = 64e-5
DEEPNORM_ALPHA = (2.0 * DEPTH) ** 0.25

LANES = 128
D_CHUNKS = D_MODEL // LANES
SCAN_CHUNK = 64
SCAN_HEADS_PER_STEP = 8
SCAN_CHUNKS_PER_STEP = 2
VMEM_LIMIT = 48 * 1024 * 1024

_NT = (((1,), (1,)), ((), ()))
_TN = (((0,), (0,)), ((), ()))


def _cparams(*sem):
    return pltpu.CompilerParams(dimension_semantics=sem, vmem_limit_bytes=VMEM_LIMIT)


def _layer_norm(z, w, b):
    mu = jnp.mean(z, axis=-1, keepdims=True)
    zc = z - mu
    var = jnp.mean(zc * zc, axis=-1, keepdims=True)
    return zc * lax.rsqrt(var + LN_EPS) * w + b


def _mm_kernel(a_ref, w_ref, o_ref):
    o_ref[...] = jnp.dot(a_ref[...].astype(BF16), w_ref[...],
                         preferred_element_type=F32).astype(o_ref.dtype)


def _matmul(a, w, out_dtype, tm, tn):
    M, K = a.shape
    N = w.shape[1]
    tm, tn = min(tm, M), min(tn, N)
    return pl.pallas_call(
        _mm_kernel,
        grid=(N // tn, M // tm),
        in_specs=[pl.BlockSpec((tm, K), lambda j, i: (i, 0)),
                  pl.BlockSpec((K, tn), lambda j, i: (0, j))],
        out_specs=pl.BlockSpec((tm, tn), lambda j, i: (i, j)),
        out_shape=jax.ShapeDtypeStruct((M, N), out_dtype),
        compiler_params=_cparams("parallel", "parallel"),
        name="matmul",
    )(a, w)


def _rwkv_prep_kernel(r_ref, k_ref, v_ref, l_ref, rp_ref, kp_ref, vp_ref, lp_ref,
                      mur_ref, muk_ref, muv_ref, mul_ref, w0_ref, a0_ref, kk_ref, ka_ref,
                      wup_ref, aup_ref, gup_ref,
                      ro_ref, lw_ref, k2_ref, vo_ref, kkr_ref, ic_ref, g_ref, *, tiles_per_seq):
    i = pl.program_id(0)
    first = (i % tiles_per_seq) == 0

    def shifted(cur_ref, prev_ref, mu_ref):
        p = cur_ref[...]
        tm = p.shape[0]
        prev_row = jnp.where(first, 0.0, prev_ref[7:8, :])
        row = lax.broadcasted_iota(jnp.int32, p.shape, 0)
        p_prev = jnp.where(row == 0, prev_row, pltpu.roll(p, 1, 0))
        del tm
        return p + (p_prev - p) * mu_ref[...]

    r = shifted(r_ref, rp_ref, mur_ref)
    k = shifted(k_ref, kp_ref, muk_ref)
    v = shifted(v_ref, vp_ref, muv_ref)
    lo = shifted(l_ref, lp_ref, mul_ref)

    wl = w0_ref[...] + jnp.dot(jnp.tanh(lo).astype(BF16), wup_ref[...], preferred_element_type=F32)
    nz = -wl
    softplus = jnp.maximum(nz, 0.0) + jnp.log(1.0 + jnp.exp(-jnp.abs(nz)))
    w_log = -softplus - 0.5
    lw_ref[...] = -jnp.exp(w_log)
    al = a0_ref[...] + jnp.dot(lo.astype(BF16), aup_ref[...], preferred_element_type=F32)
    iclr = 1.0 / (1.0 + jnp.exp(-al))
    sg = 1.0 / (1.0 + jnp.exp(-lo))
    g_ref[...] = jnp.dot(sg.astype(BF16), gup_ref[...], preferred_element_type=F32)
    ro_ref[...] = r
    vo_ref[...] = v
    kkr_ref[...] = k * kk_ref[...]
    ic_ref[...] = iclr
    k2_ref[...] = k * (1.0 + (iclr - 1.0) * ka_ref[...])


def _rwkv_prep(main, lora, S, mu_r, mu_k, mu_v, mu_l, w0, a0, k_k, k_a, wup, aup, gup):
    T = main.shape[0]
    tm = min(256, S)
    C = RWKV_WIDTH
    prev = lambda c: (lambda i: (jnp.maximum(i * (tm // 8) - 1, 0), c))
    cur = lambda c: (lambda i: (i, c))
    vec = lambda n: pl.BlockSpec((1, n), lambda i: (0, 0))
    mat = pl.BlockSpec((LORA_PAD, C), lambda i: (0, 0))
    out = pl.BlockSpec((tm, C), lambda i: (i, 0))
    return pl.pallas_call(
        functools.partial(_rwkv_prep_kernel, tiles_per_seq=S // tm),
        grid=(T // tm,),
        in_specs=[pl.BlockSpec((tm, C), cur(0)), pl.BlockSpec((tm, C), cur(1)), pl.BlockSpec((tm, C), cur(2)),
                  pl.BlockSpec((tm, LORA_PAD), cur(0)),
                  pl.BlockSpec((8, C), prev(0)), pl.BlockSpec((8, C), prev(1)), pl.BlockSpec((8, C), prev(2)),
                  pl.BlockSpec((8, LORA_PAD), prev(0)),
                  vec(C), vec(C), vec(C), vec(LORA_PAD), vec(C), vec(C), vec(C), vec(C),
                  mat, mat, mat],
        out_specs=[out] * 7,
        out_shape=[jax.ShapeDtypeStruct((T, C), F32)] * 7,
        compiler_params=_cparams("parallel"),
        name="rwkv_prep",
    )(main, main, main, lora, main, main, main, lora,
      mu_r, mu_k, mu_v, mu_l, w0, a0, k_k, k_a, wup, aup, gup)


def _rwkv_scan_kernel(r_ref, lw_ref, k_ref, v_ref, kkr_ref, ic_ref, rk_ref, lnw_ref, lnb_ref,
                      y_ref, state):
    L = SCAN_CHUNK
    N = RWKV_HEAD
    win = lambda ref, q: ref[q[1] * L:(q[1] + 1) * L, q[0] * N:(q[0] + 1) * N]

    @pl.when(pl.program_id(1) == 0)
    def _():
        state[...] = jnp.zeros_like(state)

    row = lax.broadcasted_iota(jnp.int32, (L, L), 0)
    col = lax.broadcasted_iota(jnp.int32, (L, L), 1)
    tril_incl = (col <= row).astype(F32)
    eye = (col == row).astype(F32)
    row2 = lax.broadcasted_iota(jnp.int32, (L, 2 * L), 0)
    col2 = lax.broadcasted_iota(jnp.int32, (L, 2 * L), 1) % L
    zeros_ln = jnp.zeros((L, N), BF16)
    bdot = lambda a, b: jnp.dot(a.astype(BF16), b.astype(BF16), preferred_element_type=F32)

    heads = SCAN_HEADS_PER_STEP
    pairs = [(hd, c) for c in range(SCAN_CHUNKS_PER_STEP) for hd in range(heads)]
    G = range(len(pairs))
    r = [win(r_ref, q) for q in pairs]
    lw = [win(lw_ref, q) for q in pairs]
    k = [win(k_ref, q) for q in pairs]
    v = [win(v_ref, q) for q in pairs]
    lp = [jnp.dot(tril_incl, lw[g], precision=lax.Precision.HIGHEST, preferred_element_type=F32) for g in G]
    ep = [jnp.exp(lp[g]) for g in G]
    en = [jnp.exp(-lp[g]) for g in G]
    kk = []
    for q in pairs:
        kkr = win(kkr_ref, q)
        nrm = jnp.sqrt(jnp.sum(kkr * kkr, axis=-1, keepdims=True))
        kk.append(kkr / jnp.maximum(nrm, 1e-12))
    at = [((-kk[g]) * jnp.exp(lp[g] - lw[g])).astype(BF16) for g in G]
    rt = [r[g] * ep[g] for g in G]
    bt = [((kk[g] * win(ic_ref, pairs[g])) * en[g]).astype(BF16) for g in G]
    bk = [jnp.concatenate([bt[g], (k[g] * en[g]).astype(BF16)], axis=0) for g in G]
    vb = [v[g].astype(BF16) for g in G]
    m_a = [jnp.where(col2 < row2, lax.dot_general(at[g], bk[g], _NT, preferred_element_type=F32), 0.0) for g in G]
    m_r = [jnp.where(col2 <= row2, lax.dot_general(rt[g].astype(BF16), bk[g], _NT, preferred_element_type=F32),
                     0.0).astype(BF16) for g in G]
    x = [jnp.dot(m_a[g].astype(BF16), jnp.concatenate([zeros_ln, vb[g]], axis=0), preferred_element_type=F32)
         for g in G]
    square = lambda m: [bdot(m[g], m[g]) for g in G]
    pair = lambda lo, hi: [(eye + lo[g]) + bdot(eye + lo[g], hi[g]) for g in G]
    p1 = [m_a[g][:, :L] for g in G]
    p2 = square(p1)
    f01 = pair(p1, p2)
    p4 = square(p2)
    p8 = square(p4)
    f23 = pair(p4, p8)
    f03 = [bdot(f01[g], f23[g]) for g in G]
    p16 = square(p8)
    p32 = square(p16)
    f45 = pair(p16, p32)
    tb = [bdot(f03[g], f45[g]).astype(BF16) for g in G]
    w_b = [jnp.dot(tb[g], at[g], preferred_element_type=F32).astype(BF16) for g in G]
    u0_b = [jnp.dot(tb[g], x[g].astype(BF16), preferred_element_type=F32).astype(BF16) for g in G]
    uv = [jnp.concatenate([u0_b[g], vb[g]], axis=0) for g in G]
    rp = [rt[g] + jnp.dot(m_r[g], jnp.concatenate([w_b[g], zeros_ln], axis=0), preferred_element_type=F32)
          for g in G]
    y0 = [jnp.dot(m_r[g], uv[g], preferred_element_type=F32) for g in G]
    p_last = [ep[g][L - 1:L, :] for g in G]
    g_mat = [(eye + lax.dot_general(w_b[g], bt[g], _TN, preferred_element_type=F32)) * p_last[g] for g in G]
    h_mat = [lax.dot_general(uv[g], bk[g], _TN, preferred_element_type=F32) * p_last[g] for g in G]
    st = [state[hd] for hd in range(heads)]
    y = [None] * len(pairs)
    for g in G:
        hd = pairs[g][0]
        y[g] = lax.dot_general(rp[g].astype(BF16), st[hd].astype(BF16), _NT, preferred_element_type=F32) + y0[g]
        st[hd] = bdot(st[hd], g_mat[g]) + h_mat[g]
    for hd in range(heads):
        state[hd] = st[hd]
    for g in G:
        hd, c = pairs[g]
        mu = jnp.mean(y[g], axis=-1, keepdims=True)
        yc = y[g] - mu
        var = jnp.mean(yc * yc, axis=-1, keepdims=True)
        yn = yc * lax.rsqrt(var + GN_EPS) * lnw_ref[hd] + lnb_ref[hd]
        bonus = jnp.sum(r[g] * k[g] * rk_ref[hd], axis=-1, keepdims=True) * v[g]
        y_ref[c * L:(c + 1) * L, hd * N:(hd + 1) * N] = yn + bonus


def _rwkv_scan(r, lw, k2, v, kkr, ic, rk, lnw, lnb, S):
    T = r.shape[0]
    N = RWKV_HEAD
    G = SCAN_HEADS_PER_STEP
    rows = SCAN_CHUNK * SCAN_CHUNKS_PER_STEP
    groups_per_batch = RWKV_HEADS // G
    steps = S // rows
    blk = pl.BlockSpec((rows, G * N), lambda i, c: (i // groups_per_batch * steps + c, i % groups_per_batch))
    par = pl.BlockSpec((G, 1, N), lambda i, c: (i % groups_per_batch, 0, 0))
    return pl.pallas_call(
        _rwkv_scan_kernel,
        grid=(T // S * groups_per_batch, steps),
        in_specs=[blk] * 6 + [par] * 3,
        out_specs=blk,
        out_shape=jax.ShapeDtypeStruct((T, RWKV_WIDTH), F32),
        scratch_shapes=[pltpu.VMEM((G, N, N), F32)],
        compiler_params=_cparams("parallel", "arbitrary"),
        name="rwkv_scan",
    )(r, lw, k2, v, kkr, ic, rk, lnw, lnb)


def _rope_kernel(q_ref, k_ref, v_ref, pos_ref, qo_ref, ko_ref, vo_ref):
    tm = q_ref.shape[0]
    lane = lax.broadcasted_iota(jnp.int32, (1, LANES), 1)
    d = lane % DIFF_QK
    half = ROT_DIMS // 2
    freq = jnp.exp((d % half).astype(F32) * (-2.0 * math.log(ROPE_THETA) / ROT_DIMS))
    freq = jnp.where(d < ROT_DIMS, freq, 0.0)
    ang = pos_ref[...].astype(F32) * freq
    cos = jnp.cos(ang)
    sin = jnp.sin(ang)
    c_mul = jnp.where(d < ROT_DIMS, cos, 1.0)
    s_lo = jnp.where(d < half, -sin, 0.0)
    s_hi = jnp.where((d >= half) & (d < ROT_DIMS), sin, 0.0)
    del tm

    def rot(t):
        return t * c_mul + pltpu.roll(t, LANES - half, 1) * s_lo + pltpu.roll(t, half, 1) * s_hi

    scale = DIFF_QK ** -0.5
    for h in range(DIFF_HEADS):
        sl = slice(h * LANES, (h + 1) * LANES)
        qo_ref[:, sl] = (rot(q_ref[:, sl]) * scale).astype(BF16)
        ko_ref[:, sl] = rot(k_ref[:, sl]).astype(BF16)
    vo_ref[...] = v_ref[...].astype(BF16)


def _rope(main, pos):
    T = main.shape[0]
    tm = min(256, T)
    C = DIFF_WIDTH
    base = 3 * RWKV_WIDTH // C
    blk = lambda c: pl.BlockSpec((tm, C), lambda i: (i, base + c))
    out = pl.BlockSpec((tm, C), lambda i: (i, 0))
    return pl.pallas_call(
        _rope_kernel,
        grid=(T // tm,),
        in_specs=[blk(0), blk(1), blk(2), pl.BlockSpec((tm, 1), lambda i: (i, 0))],
        out_specs=[out] * 3,
        out_shape=[jax.ShapeDtypeStruct((T, C), BF16)] * 3,
        compiler_params=_cparams("parallel"),
        name="rope",
    )(main, main, main, pos)


def _flash_kernel(q_ref, k_ref, v_ref, lq1_ref, lk1_ref, lq2_ref, lk2_ref, sub_ref, o_ref, *, lam_init, tq):
    i = pl.program_id(2)
    tk = tq
    q = q_ref[...]
    lane = lax.broadcasted_iota(jnp.int32, q.shape, 1)
    zero = jnp.zeros_like(q)
    qq = jnp.concatenate([jnp.where(lane < DIFF_QK, q, zero), jnp.where(lane >= DIFF_QK, q, zero)], axis=0)

    def block(j, carry, diagonal):
        m_prev, l_prev, acc = carry
        off = pl.multiple_of(j * tk, tk)
        k = k_ref[pl.ds(off, tk), :]
        v = v_ref[pl.ds(off, tk), :]
        s = lax.dot_general(qq, k, _NT, preferred_element_type=F32)
        if diagonal:
            row = lax.broadcasted_iota(jnp.int32, (2 * tq, tk), 0)
            col = lax.broadcasted_iota(jnp.int32, (2 * tq, tk), 1)
            s = jnp.where(col <= jnp.where(row >= tq, row - tq, row), s, -jnp.inf)
        m_new = jnp.maximum(m_prev, jnp.max(s, axis=1, keepdims=True))
        a = jnp.exp(m_prev - m_new)
        p = jnp.exp(s - m_new)
        l_new = a * l_prev + jnp.sum(p, axis=1, keepdims=True)
        acc = a * acc + jnp.dot(p.astype(BF16), v, preferred_element_type=F32)
        return m_new, l_new, acc

    init = (jnp.full((2 * tq, 1), -jnp.inf, F32), jnp.zeros((2 * tq, 1), F32), jnp.zeros((2 * tq, LANES), F32))
    carry = lax.fori_loop(0, i, lambda j, c: block(j, c, False), init)
    _, l_fin, acc = block(i, carry, True)
    lam = (jnp.exp(jnp.sum(lq1_ref[...] * lk1_ref[...], keepdims=True))
           - jnp.exp(jnp.sum(lq2_ref[...] * lk2_ref[...], keepdims=True)) + lam_init)
    on = acc / l_fin
    o = on[:tq] - lam * on[tq:]
    o = o * lax.rsqrt(jnp.mean(o * o, axis=-1, keepdims=True) + LN_EPS) * sub_ref[...]
    o_ref[...] = o * (1.0 - lam_init)


def _diff_attention(q, k, v, B, S, lq1, lk1, lq2, lk2, subw, layer_idx):
    T = q.shape[0]
    tq = min(1024, S)
    nq = S // tq
    lam_init = 0.8 - 0.6 * math.exp(-0.3 * layer_idx)
    vec = pl.BlockSpec((1, DIFF_QK), lambda b, h, i: (0, 0))
    seq = pl.BlockSpec((S, LANES), lambda b, h, i: (b, h))
    return pl.pallas_call(
        functools.partial(_flash_kernel, lam_init=lam_init, tq=tq),
        grid=(B, DIFF_HEADS, nq),
        in_specs=[pl.BlockSpec((tq, LANES), lambda b, h, i: (b * nq + i, h)), seq, seq,
                  vec, vec, vec, vec,
                  pl.BlockSpec((1, DIFF_VHEAD), lambda b, h, i: (0, 0))],
        out_specs=pl.BlockSpec((tq, LANES), lambda b, h, i: (b * nq + i, h)),
        out_shape=jax.ShapeDtypeStruct((T, DIFF_WIDTH), F32),
        compiler_params=_cparams("parallel", "parallel", "parallel"),
        name="diff_attention",
    )(q, k, v, lq1, lk1, lq2, lk2, subw)


def _outproj_kernel(x_ref, ya_ref, g_ref, yb_ref, wa_ref, wb_ref, lnw_ref, lnb_ref, o_ref):
    ya = (ya_ref[...] * g_ref[...]).astype(BF16)
    mix = (jnp.dot(ya, wa_ref[...], preferred_element_type=F32)
           + jnp.dot(yb_ref[...].astype(BF16), wb_ref[...], preferred_element_type=F32))
    o_ref[...] = _layer_norm(DEEPNORM_ALPHA * x_ref[...] + mix, lnw_ref[...], lnb_ref[...])


def _outproj(x, ya, g, yb, wa, wb, lnw, lnb):
    T, D = x.shape
    tm = min(256, T)
    half = pl.BlockSpec((tm, RWKV_WIDTH), lambda i: (i, 0))
    full = pl.BlockSpec((tm, D), lambda i: (i, 0))
    wsp = pl.BlockSpec((RWKV_WIDTH, D), lambda i: (0, 0))
    vec = pl.BlockSpec((1, D), lambda i: (0, 0))
    return pl.pallas_call(
        _outproj_kernel,
        grid=(T // tm,),
        in_specs=[full, half, half, half, wsp, wsp, vec, vec],
        out_specs=full,
        out_shape=jax.ShapeDtypeStruct((T, D), F32),
        compiler_params=_cparams("parallel"),
        name="outproj_ln",
    )(x, ya, g, yb, wa, wb, lnw, lnb)


def _xattn_kernel(x_ref, wq_ref, k_ref, v_ref, wo_ref, lnw_ref, lnb_ref, o_ref):
    x = x_ref[...]
    q = jnp.dot(x.astype(BF16), wq_ref[...], preferred_element_type=F32) * (XATTN_HEAD ** -0.5)
    outs = []
    for h in range(XATTN_HEADS):
        sl = slice(h * XATTN_HEAD, (h + 1) * XATTN_HEAD)
        s = lax.dot_general(q[:, sl].astype(BF16), k_ref[:, sl], _NT, preferred_element_type=F32)
        p = jnp.exp(s - jnp.max(s, axis=-1, keepdims=True))
        p = p / jnp.sum(p, axis=-1, keepdims=True)
        outs.append(jnp.dot(p.astype(BF16), v_ref[:, sl], preferred_element_type=F32))
    o = jnp.concatenate(outs, axis=1).astype(BF16)
    xa = jnp.dot(o, wo_ref[...], preferred_element_type=F32)
    o_ref[...] = _layer_norm(DEEPNORM_ALPHA * x + xa, lnw_ref[...], lnb_ref[...])


def _xattn(x, S, wq, kmem, vmem, wo, lnw, lnb, mem_len):
    T, D = x.shape
    tm = min(256, S)
    tiles_per_seq = S // tm
    full = pl.BlockSpec((tm, D), lambda i: (i, 0))
    kv = pl.BlockSpec((mem_len, XATTN_WIDTH), lambda i: (i // tiles_per_seq, 0))
    vec = pl.BlockSpec((1, D), lambda i: (0, 0))
    return pl.pallas_call(
        _xattn_kernel,
        grid=(T // tm,),
        in_specs=[full, pl.BlockSpec((D, XATTN_WIDTH), lambda i: (0, 0)), kv, kv,
                  pl.BlockSpec((XATTN_WIDTH, D), lambda i: (0, 0)), vec, vec],
        out_specs=full,
        out_shape=jax.ShapeDtypeStruct((T, D), F32),
        compiler_params=_cparams("parallel"),
        name="xattn_ln",
    )(x, wq, kmem, vmem, wo, lnw, lnb)


ROUTE_HEADS_PER_ITER = 4


def _route_kernel(x_ref, pq_ref, sk_ref, e_ref, g_ref, q_s, sv_s, si_s, cand_s, eid_s):
    tm = x_ref.shape[0]
    K = PEER_TOPK
    NK = PEER_KEYS
    q = jnp.dot(x_ref[...].astype(BF16), pq_ref[...], preferred_element_type=F32)
    for hc in range(2 * PEER_HEADS):
        q_s[hc] = q[:, hc * LANES:(hc + 1) * LANES].astype(BF16)
    neg = -jnp.inf
    n_b = [K // (a + 1) for a in range(K)]
    offs = [sum(n_b[:a]) for a in range(K)]
    n_cand = cand_s.shape[1]
    P = ROUTE_HEADS_PER_ITER

    for sub in range(tm // LANES):
        tok = slice(sub * LANES, (sub + 1) * LANES)
        rowk = lax.broadcasted_iota(jnp.int32, (NK, LANES), 0).astype(F32)
        rowc = lax.broadcasted_iota(jnp.int32, (n_cand, LANES), 0).astype(F32)

        def heads(hp, carry):
            for p in range(P):
                s = [lax.dot_general(sk_ref[c], q_s[2 * (hp * P + p) + c, tok, :], _NT,
                                     preferred_element_type=F32) for c in range(2)]
                for j in range(K):
                    for c in range(2):
                        m = jnp.max(s[c], axis=0, keepdims=True)
                        idx = jnp.min(jnp.where(s[c] == m, rowk, float(NK)), axis=0, keepdims=True)
                        s[c] = jnp.where(rowk == idx, neg, s[c])
                        sv_s[p, c, j:j + 1, :] = m
                        si_s[p, c, j:j + 1, :] = idx
                cand_s[p, n_cand - 8:, :] = jnp.full((8, LANES), neg, F32)
                eid_s[p, n_cand - 8:, :] = jnp.zeros((8, LANES), F32)
                for a in range(K):
                    rows = slice(offs[a], offs[a] + n_b[a])
                    cand_s[p, rows, :] = sv_s[p, 0, a:a + 1, :] + sv_s[p, 1, 0:n_b[a], :]
                    eid_s[p, rows, :] = si_s[p, 0, a:a + 1, :] * float(NK) + si_s[p, 1, 0:n_b[a], :]
            cand = [cand_s[p] for p in range(P)]
            eid = [eid_s[p] for p in range(P)]
            fv = [[] for _ in range(P)]
            fe = [[] for _ in range(P)]
            for j in range(K):
                for p in range(P):
                    m = jnp.max(cand[p], axis=0, keepdims=True)
                    idx = jnp.min(jnp.where(cand[p] == m, rowc, float(n_cand)), axis=0, keepdims=True)
                    sel = rowc == idx
                    fe[p].append(jnp.max(jnp.where(sel, eid[p], -1.0), axis=0, keepdims=True))
                    cand[p] = jnp.where(sel, neg, cand[p])
                    fv[p].append(m)
            for p in range(P):
                w = [jnp.exp(fv[p][j] - fv[p][0]) for j in range(K)]
                den = w[0]
                for j in range(1, K):
                    den = den + w[j]
                for j in range(K):
                    sv_s[p, 0, j:j + 1, :] = w[j] / den
                    si_s[p, 0, j:j + 1, :] = fe[p][j]
                base = pl.multiple_of((hp * P + p) * K, K)
                g_ref[pl.ds(base, K), tok] = sv_s[p, 0]
                e_ref[pl.ds(base, K), tok] = si_s[p, 0].astype(jnp.int32)
            return carry

        lax.fori_loop(0, PEER_HEADS // P, heads, 0)


def _route(x, pq, sk):
    T, D = x.shape
    tm = min(256, T)
    out = pl.BlockSpec((PEER_SLOTS, tm), lambda i: (0, i))
    n_cand = sum(PEER_TOPK // (a + 1) for a in range(PEER_TOPK))
    n_cand = -(-n_cand // 8) * 8
    return pl.pallas_call(
        _route_kernel,
        grid=(T // tm,),
        in_specs=[pl.BlockSpec((tm, D), lambda i: (i, 0)),
                  pl.BlockSpec((D, D), lambda i: (0, 0)),
                  pl.BlockSpec((2, PEER_KEYS, LANES), lambda i: (0, 0, 0))],
        out_specs=[out, out],
        out_shape=[jax.ShapeDtypeStruct((PEER_SLOTS, T), jnp.int32),
                   jax.ShapeDtypeStruct((PEER_SLOTS, T), F32)],
        scratch_shapes=[pltpu.VMEM((2 * PEER_HEADS, tm, LANES), BF16),
                        pltpu.VMEM((ROUTE_HEADS_PER_ITER, 2, PEER_TOPK, LANES), F32),
                        pltpu.VMEM((ROUTE_HEADS_PER_ITER, 2, PEER_TOPK, LANES), F32),
                        pltpu.VMEM((ROUTE_HEADS_PER_ITER, n_cand, LANES), F32),
                        pltpu.VMEM((ROUTE_HEADS_PER_ITER, n_cand, LANES), F32)],
        compiler_params=_cparams("parallel"),
        name="peer_route",
    )(x, pq, sk)


EXPERT_TOKENS_PER_STEP = 128
EXPERT_GROUP = 4
EXPERT_SETS = 4
EXPERT_AHEAD = 2


def _expert_kernel(idx_ref, nidx_ref, x_ref, gt_ref, lnw_ref, lnb_ref, uv_ref, o_ref, *scratch):
    G = EXPERT_GROUP
    nbuf = G * EXPERT_SETS
    bufs, sem = scratch[:nbuf], scratch[nbuf]
    TB = x_ref.shape[0]
    NE = PEER_SLOTS
    step = pl.program_id(0)
    per_chunk = NE // (2 * D_CHUNKS)

    def start(iref, t, e, b):
        priority = e % 2 if isinstance(e, int) else 0
        pltpu.make_async_copy(uv_ref.at[iref[e, t]], bufs[b].at[:, e, :], sem.at[b]).start(priority=priority)

    def wait(b):
        pltpu.make_async_copy(bufs[b], bufs[b], sem.at[b]).wait()

    @pl.when(step == 0)
    def _():
        for b in range(EXPERT_AHEAD * G):
            def body(e, c, b=b):
                start(idx_ref, b, e, b)
                return c
            lax.fori_loop(0, NE, body, 0)

    lane = lax.broadcasted_iota(jnp.int32, (NE, TB), 1)

    def group(t0, s, issue):
        for k in range(G):
            wait(s * G + k)
        xrow = [x_ref[pl.ds(t0 + k, 1), :] for k in range(G)]

        def u_phase(k):
            buf = bufs[s * G + k]
            acc = jnp.zeros((NE, LANES), F32)
            for c in range(D_CHUNKS):
                for e in range(c * per_chunk, (c + 1) * per_chunk):
                    issue(k, e)
                u = lax.bitcast_convert_type(buf[c] & jnp.uint32(0xFFFF0000), F32)
                acc = acc + u * xrow[k][:, c * LANES:(c + 1) * LANES]
            h = jnp.sum(acc, axis=1, keepdims=True)
            gate = jnp.sum(jnp.where(lane == t0 + k, gt_ref[...], 0.0), axis=1, keepdims=True)
            return 0.5 * h * (1.0 + lax.erf(h * (2.0 ** -0.5))) * gate

        def v_phase(k, a):
            buf = bufs[s * G + k]
            ys = []
            for c in range(D_CHUNKS):
                for e in range((D_CHUNKS + c) * per_chunk, (D_CHUNKS + c + 1) * per_chunk):
                    issue(k, e)
                v = lax.bitcast_convert_type(buf[c] << 16, F32)
                ys.append(jnp.sum(v * a, axis=0, keepdims=True))
            z = DEEPNORM_ALPHA * xrow[k] + jnp.concatenate(ys, axis=1)
            o_ref[t0 + k] = _layer_norm(z, lnw_ref[...], lnb_ref[...])

        a = u_phase(0)
        for k in range(G):
            a_next = u_phase(k + 1) if k + 1 < G else None
            v_phase(k, a)
            a = a_next

    def ring(t0, last):
        for s in range(EXPERT_SETS):
            ns = (s + EXPERT_AHEAD) % EXPERT_SETS
            if last and s + EXPERT_AHEAD >= EXPERT_SETS:
                issue = lambda k, e, g0=(s + EXPERT_AHEAD - EXPERT_SETS) * G, ns=ns: start(
                    nidx_ref, g0 + k, e, ns * G + k)
            else:
                issue = lambda k, e, g0=t0 + (s + EXPERT_AHEAD) * G, ns=ns: start(idx_ref, g0 + k, e, ns * G + k)
            group(t0 + s * G, s, issue)

    def ring_body(q, c):
        ring(q * (G * EXPERT_SETS), False)
        return c

    lax.fori_loop(0, TB // (G * EXPERT_SETS) - 1, ring_body, 0)
    ring(TB - G * EXPERT_SETS, True)

    @pl.when(step == pl.num_programs(0) - 1)
    def _():
        for b in range(EXPERT_AHEAD * G):
            wait(b)


def _experts(exp_t, x, gates_t, lnw, lnb, uv):
    T, D = x.shape
    TB = min(EXPERT_TOKENS_PER_STEP, T)
    NE = PEER_SLOTS
    n = T // TB
    nbuf = EXPERT_GROUP * EXPERT_SETS
    out = pl.pallas_call(
        _expert_kernel,
        grid=(n,),
        in_specs=[pl.BlockSpec((NE, TB), lambda i: (0, i), memory_space=pltpu.SMEM),
                  pl.BlockSpec((NE, TB), lambda i: (0, jnp.minimum(i + 1, n - 1)), memory_space=pltpu.SMEM),
                  pl.BlockSpec((TB, D), lambda i: (i, 0)),
                  pl.BlockSpec((NE, TB), lambda i: (0, i)),
                  pl.BlockSpec((1, D), lambda i: (0, 0)),
                  pl.BlockSpec((1, D), lambda i: (0, 0)),
                  pl.BlockSpec(memory_space=pl.ANY)],
        out_specs=pl.BlockSpec((TB, 1, D), lambda i: (i, 0, 0)),
        out_shape=jax.ShapeDtypeStruct((T, 1, D), F32),
        scratch_shapes=[pltpu.VMEM((D_CHUNKS, NE, LANES), jnp.uint32)] * nbuf
        + [pltpu.SemaphoreType.DMA((nbuf,))],
        compiler_params=_cparams("arbitrary"),
        name="peer_experts",
    )(exp_t, exp_t, x, gates_t, lnw, lnb, uv)
    return out.reshape(T, D)


def kernel(x, mem, positions, w_in, shift_mu, w0, w_up, a0, a_up, g_up, k_k, k_a, r_k, lnx_w, lnx_b, lam_q1, lam_k1, lam_q2, lam_k2, subln_w, w_out, ln1_w, ln1_b, xq, xk, xv, xo, ln2_w, ln2_b, pq, subkeys, peer_u, peer_v, ln3_w, ln3_b):
    B, S, D = x.shape
    T = B * S
    M = mem.shape[1]
    H, N, C = RWKV_HEADS, RWKV_HEAD, RWKV_WIDTH
    xt = x.reshape(T, D)
    memt = mem.reshape(B * M, D)
    pos = positions.reshape(T, 1)
    row = lambda a: a.reshape(1, -1)
    lora_pad = LORA_PAD - LORA_COLS

    for l in range(DEPTH):
        wl = w_in[l]
        w_main = jnp.concatenate([wl[:, :3 * C], wl[:, RWKV_COLS:]], axis=1).astype(BF16)
        w_lora = jnp.pad(wl[:, 3 * C:RWKV_COLS], ((0, 0), (0, lora_pad))).astype(BF16)
        mu = shift_mu[l]
        mu_l = jnp.pad(mu[3 * C:], (0, lora_pad))
        o1, o2 = DECAY_LORA, DECAY_LORA + AAA_LORA
        wup = jnp.pad(w_up[l], ((0, LORA_PAD - o1), (0, 0))).astype(BF16)
        aup = jnp.pad(a_up[l], ((o1, LORA_PAD - o2), (0, 0))).astype(BF16)
        gup = jnp.pad(g_up[l], ((o2, lora_pad), (0, 0))).astype(BF16)

        main = _matmul(xt, w_main, F32, 512, 1024)
        lora = _matmul(xt, w_lora, F32, 512, LORA_PAD)
        r, lw, k2, v, kkr, ic, g = _rwkv_prep(
            main, lora, S, row(mu[:C]), row(mu[C:2 * C]), row(mu[2 * C:3 * C]), row(mu_l),
            row(w0[l]), row(a0[l]), row(k_k[l]), row(k_a[l]), wup, aup, gup)
        ya = _rwkv_scan(r, lw, k2, v, kkr, ic,
                        r_k[l].reshape(H, 1, N), lnx_w[l].reshape(H, 1, N), lnx_b[l].reshape(H, 1, N), S)

        qr, kr, vb = _rope(main, pos)
        yb = _diff_attention(qr, kr, vb, B, S, row(lam_q1[l]), row(lam_k1[l]), row(lam_q2[l]), row(lam_k2[l]),
                             row(subln_w[l]), l)

        wo = w_out[l].astype(BF16)
        x1 = _outproj(xt, ya, g, yb, wo[:C], wo[C:], row(ln1_w[l]), row(ln1_b[l]))

        kmem = _matmul(memt, xk[l].astype(BF16), BF16, 512, XATTN_WIDTH)
        vmem = _matmul(memt, xv[l].astype(BF16), BF16, 512, XATTN_WIDTH)
        x2 = _xattn(x1, S, xq[l].astype(BF16), kmem, vmem, xo[l].astype(BF16), row(ln2_w[l]), row(ln2_b[l]), M)

        exp_t, gates_t = _route(x2, pq[l].astype(BF16), subkeys[l].astype(BF16))
        half = lambda w: lax.bitcast_convert_type(w.astype(BF16), jnp.uint16).astype(jnp.uint32)
        uv = ((half(peer_u[l]) << 16) | half(peer_v[l])).reshape(PEER_EXPERTS, D_CHUNKS, LANES)
        xt = _experts(exp_t, x2, gates_t, row(ln3_w[l]), row(ln3_b[l]), uv)
    return xt.reshape(B, S, D)
```

```python
import functools
import math

import jax
import jax.numpy as jnp
from jax import lax
from jax.experimental import pallas as pl
from jax.experimental.pallas import tpu as pltpu

F32 = jnp.float32
BF16 = jnp.bfloat16

D_MODEL = 2048
DEPTH = 2
RWKV_WIDTH = 1024
RWKV_HEAD = 64
RWKV_HEADS = 16
DECAY_LORA = 64
AAA_LORA = 64
GATE_LORA = 160
LORA_COLS = DECAY_LORA + AAA_LORA + GATE_LORA
LORA_PAD = 384
RWKV_COLS = 3 * RWKV_WIDTH + LORA_COLS
DIFF_WIDTH = 1024
DIFF_VHEAD = 128
DIFF_HEADS = 8
DIFF_QK = 64
ROT_DIMS = 16
ROPE_THETA = 500000.0
XATTN_HEADS = 4
XATTN_HEAD = 128
XATTN_WIDTH = 512
PEER_HEADS = 8
PEER_KEYS = 128
PEER_EXPERTS = PEER_KEYS * PEER_KEYS
PEER_TOPK = 16
PEER_SLOTS = PEER_HEADS * PEER_TOPK
LN_EPS = 1e-5
GN_EPS = 64e-5
DEEPNORM_ALPHA = (2.0 * DEPTH) ** 0.25

LANES = 128
D_CHUNKS = D_MODEL // LANES
SCAN_CHUNK = 64
SCAN_HEADS_PER_STEP = 8
SCAN_CHUNKS_PER_STEP = 2
VMEM_LIMIT = 48 * 1024 * 1024

_NT = (((1,), (1,)), ((), ()))
_TN = (((0,), (0,)), ((), ()))


def _cparams(*sem):
    return pltpu.CompilerParams(dimension_semantics=sem, vmem_limit_bytes=VMEM_LIMIT)


def _layer_norm(z, w, b):
    mu = jnp.mean(z, axis=-1, keepdims=True)
    zc = z - mu
    var = jnp.mean(zc * zc, axis=-1, keepdims=True)
    return zc * lax.rsqrt(var + LN_EPS) * w + b


def _mm_kernel(a_ref, w_ref, o_ref):
    o_ref[...] = jnp.dot(a_ref[...].astype(BF16), w_ref[...],
                         preferred_element_type=F32).astype(o_ref.dtype)


def _matmul(a, w, out_dtype, tm, tn):
    M, K = a.shape
    N = w.shape[1]
    tm, tn = min(tm, M), min(tn, N)
    return pl.pallas_call(
        _mm_kernel,
        grid=(N // tn, M // tm),
        in_specs=[pl.BlockSpec((tm, K), lambda j, i: (i, 0)),
                  pl.BlockSpec((K, tn), lambda j, i: (0, j))],
        out_specs=pl.BlockSpec((tm, tn), lambda j, i: (i, j)),
        out_shape=jax.ShapeDtypeStruct((M, N), out_dtype),
        compiler_params=_cparams("parallel", "parallel"),
        name="matmul",
    )(a, w)


def _rwkv_prep_kernel(r_ref, k_ref, v_ref, l_ref, rp_ref, kp_ref, vp_ref, lp_ref,
                      mur_ref, muk_ref, muv_ref, mul_ref, w0_ref, a0_ref, kk_ref, ka_ref,
                      wup_ref, aup_ref, gup_ref,
                      ro_ref, lw_ref, k2_ref, vo_ref, kkr_ref, ic_ref, g_ref, *, tiles_per_seq):
    i = pl.program_id(0)
    first = (i % tiles_per_seq) == 0

    def shifted(cur_ref, prev_ref, mu_ref):
        p = cur_ref[...]
        tm = p.shape[0]
        prev_row = jnp.where(first, 0.0, prev_ref[7:8, :])
        row = lax.broadcasted_iota(jnp.int32, p.shape, 0)
        p_prev = jnp.where(row == 0, prev_row, pltpu.roll(p, 1, 0))
        del tm
        return p + (p_prev - p) * mu_ref[...]

    r = shifted(r_ref, rp_ref, mur_ref)
    k = shifted(k_ref, kp_ref, muk_ref)
    v = shifted(v_ref, vp_ref, muv_ref)
    lo = shifted(l_ref, lp_ref, mul_ref)

    wl = w0_ref[...] + jnp.dot(jnp.tanh(lo).astype(BF16), wup_ref[...], preferred_element_type=F32)
    nz = -wl
    softplus = jnp.maximum(nz, 0.0) + jnp.log(1.0 + jnp.exp(-jnp.abs(nz)))
    w_log = -softplus - 0.5
    lw_ref[...] = -jnp.exp(w_log)
    al = a0_ref[...] + jnp.dot(lo.astype(BF16), aup_ref[...], preferred_element_type=F32)
    iclr = 1.0 / (1.0 + jnp.exp(-al))
    sg = 1.0 / (1.0 + jnp.exp(-lo))
    g_ref[...] = jnp.dot(sg.astype(BF16), gup_ref[...], preferred_element_type=F32)
    ro_ref[...] = r
    vo_ref[...] = v
    kkr_ref[...] = k * kk_ref[...]
    ic_ref[...] = iclr
    k2_ref[...] = k * (1.0 + (iclr - 1.0) * ka_ref[...])


def _rwkv_prep(main, lora, S, mu_r, mu_k, mu_v, mu_l, w0, a0, k_k, k_a, wup, aup, gup):
    T = main.shape[0]
    tm = min(256, S)
    C = RWKV_WIDTH
    prev = lambda c: (lambda i: (jnp.maximum(i * (tm // 8) - 1, 0), c))
    cur = lambda c: (lambda i: (i, c))
    vec = lambda n: pl.BlockSpec((1, n), lambda i: (0, 0))
    mat = pl.BlockSpec((LORA_PAD, C), lambda i: (0, 0))
    out = pl.BlockSpec((tm, C), lambda i: (i, 0))
    return pl.pallas_call(
        functools.partial(_rwkv_prep_kernel, tiles_per_seq=S // tm),
        grid=(T // tm,),
        in_specs=[pl.BlockSpec((tm, C), cur(0)), pl.BlockSpec((tm, C), cur(1)), pl.BlockSpec((tm, C), cur(2)),
                  pl.BlockSpec((tm, LORA_PAD), cur(0)),
                  pl.BlockSpec((8, C), prev(0)), pl.BlockSpec((8, C), prev(1)), pl.BlockSpec((8, C), prev(2)),
                  pl.BlockSpec((8, LORA_PAD), prev(0)),
                  vec(C), vec(C), vec(C), vec(LORA_PAD), vec(C), vec(C), vec(C), vec(C),
                  mat, mat, mat],
        out_specs=[out] * 7,
        out_shape=[jax.ShapeDtypeStruct((T, C), F32)] * 7,
        compiler_params=_cparams("parallel"),
        name="rwkv_prep",
    )(main, main, main, lora, main, main, main, lora,
      mu_r, mu_k, mu_v, mu_l, w0, a0, k_k, k_a, wup, aup, gup)


def _rwkv_scan_kernel(r_ref, lw_ref, k_ref, v_ref, kkr_ref, ic_ref, rk_ref, lnw_ref, lnb_ref,
                      y_ref, state):
    L = SCAN_CHUNK
    N = RWKV_HEAD
    win = lambda ref, q: ref[q[1] * L:(q[1] + 1) * L, q[0] * N:(q[0] + 1) * N]

    @pl.when(pl.program_id(1) == 0)
    def _():
        state[...] = jnp.zeros_like(state)

    row = lax.broadcasted_iota(jnp.int32, (L, L), 0)
    col = lax.broadcasted_iota(jnp.int32, (L, L), 1)
    tril_incl = (col <= row).astype(F32)
    eye = (col == row).astype(F32)
    row2 = lax.broadcasted_iota(jnp.int32, (L, 2 * L), 0)
    col2 = lax.broadcasted_iota(jnp.int32, (L, 2 * L), 1) % L
    zeros_ln = jnp.zeros((L, N), BF16)
    bdot = lambda a, b: jnp.dot(a.astype(BF16), b.astype(BF16), preferred_element_type=F32)

    heads = SCAN_HEADS_PER_STEP
    pairs = [(hd, c) for c in range(SCAN_CHUNKS_PER_STEP) for hd in range(heads)]
    G = range(len(pairs))
    r = [win(r_ref, q) for q in pairs]
    lw = [win(lw_ref, q) for q in pairs]
    k = [win(k_ref, q) for q in pairs]
    v = [win(v_ref, q) for q in pairs]
    lp = [jnp.dot(tril_incl, lw[g], precision=lax.Precision.HIGHEST, preferred_element_type=F32) for g in G]
    ep = [jnp.exp(lp[g]) for g in G]
    en = [jnp.exp(-lp[g]) for g in G]
    kk = []
    for q in pairs:
        kkr = win(kkr_ref, q)
        nrm = jnp.sqrt(jnp.sum(kkr * kkr, axis=-1, keepdims=True))
        kk.append(kkr / jnp.maximum(nrm, 1e-12))
    at = [((-kk[g]) * jnp.exp(lp[g] - lw[g])).astype(BF16) for g in G]
    rt = [r[g] * ep[g] for g in G]
    bt = [((kk[g] * win(ic_ref, pairs[g])) * en[g]).astype(BF16) for g in G]
    bk = [jnp.concatenate([bt[g], (k[g] * en[g]).astype(BF16)], axis=0) for g in G]
    vb = [v[g].astype(BF16) for g in G]
    m_a = [jnp.where(col2 < row2, lax.dot_general(at[g], bk[g], _NT, preferred_element_type=F32), 0.0) for g in G]
    m_r = [jnp.where(col2 <= row2, lax.dot_general(rt[g].astype(BF16), bk[g], _NT, preferred_element_type=F32),
                     0.0).astype(BF16) for g in G]
    x = [jnp.dot(m_a[g].astype(BF16), jnp.concatenate([zeros_ln, vb[g]], axis=0), preferred_element_type=F32)
         for g in G]
    square = lambda m: [bdot(m[g], m[g]) for g in G]
    pair = lambda lo, hi: [(eye + lo[g]) + bdot(eye + lo[g], hi[g]) for g in G]
    p1 = [m_a[g][:, :L] for g in G]
    p2 = square(p1)
    f01 = pair(p1, p2)
    p4 = square(p2)
    p8 = square(p4)
    f23 = pair(p4, p8)
    f03 = [bdot(f01[g], f23[g]) for g in G]
    p16 = square(p8)
    p32 = square(p16)
    f45 = pair(p16, p32)
    tb = [bdot(f03[g], f45[g]).astype(BF16) for g in G]
    w_b = [jnp.dot(tb[g], at[g], preferred_element_type=F32).astype(BF16) for g in G]
    u0_b = [jnp.dot(tb[g], x[g].astype(BF16), preferred_element_type=F32).astype(BF16) for g in G]
    uv = [jnp.concatenate([u0_b[g], vb[g]], axis=0) for g in G]
    rp = [rt[g] + jnp.dot(m_r[g], jnp.concatenate([w_b[g], zeros_ln], axis=0), preferred_element_type=F32)
          for g in G]
    y0 = [jnp.dot(m_r[g], uv[g], preferred_element_type=F32) for g in G]
    p_last = [ep[g][L - 1:L, :] for g in G]
    g_mat = [(eye + lax.dot_general(w_b[g], bt[g], _TN, preferred_element_type=F32)) * p_last[g] for g in G]
    h_mat = [lax.dot_general(uv[g], bk[g], _TN, preferred_element_type=F32) * p_last[g] for g in G]
    st = [state[hd] for hd in range(heads)]
    y = [None] * len(pairs)
    for g in G:
        hd = pairs[g][0]
        y[g] = lax.dot_general(rp[g].astype(BF16), st[hd].astype(BF16), _NT, preferred_element_type=F32) + y0[g]
        st[hd] = bdot(st[hd], g_mat[g]) + h_mat[g]
    for hd in range(heads):
        state[hd] = st[hd]
    for g in G:
        hd, c = pairs[g]
        mu = jnp.mean(y[g], axis=-1, keepdims=True)
        yc = y[g] - mu
        var = jnp.mean(yc * yc, axis=-1, keepdims=True)
        yn = yc * lax.rsqrt(var + GN_EPS) * lnw_ref[hd] + lnb_ref[hd]
        bonus = jnp.sum(r[g] * k[g] * rk_ref[hd], axis=-1, keepdims=True) * v[g]
        y_ref[c * L:(c + 1) * L, hd * N:(hd + 1) * N] = yn + bonus


def _rwkv_scan(r, lw, k2, v, kkr, ic, rk, lnw, lnb, S):
    T = r.shape[0]
    N = RWKV_HEAD
    G = SCAN_HEADS_PER_STEP
    rows = SCAN_CHUNK * SCAN_CHUNKS_PER_STEP
    groups_per_batch = RWKV_HEADS // G
    steps = S // rows
    blk = pl.BlockSpec((rows, G * N), lambda i, c: (i // groups_per_batch * steps + c, i % groups_per_batch))
    par = pl.BlockSpec((G, 1, N), lambda i, c: (i % groups_per_batch, 0, 0))
    return pl.pallas_call(
        _rwkv_scan_kernel,
        grid=(T // S * groups_per_batch, steps),
        in_specs=[blk] * 6 + [par] * 3,
        out_specs=blk,
        out_shape=jax.ShapeDtypeStruct((T, RWKV_WIDTH), F32),
        scratch_shapes=[pltpu.VMEM((G, N, N), F32)],
        compiler_params=_cparams("parallel", "arbitrary"),
        name="rwkv_scan",
    )(r, lw, k2, v, kkr, ic, rk, lnw, lnb)


def _rope_kernel(q_ref, k_ref, v_ref, pos_ref, qo_ref, ko_ref, vo_ref):
    tm = q_ref.shape[0]
    lane = lax.broadcasted_iota(jnp.int32, (1, LANES), 1)
    d = lane % DIFF_QK
    half = ROT_DIMS // 2
    freq = jnp.exp((d % half).astype(F32) * (-2.0 * math.log(ROPE_THETA) / ROT_DIMS))
    freq = jnp.where(d < ROT_DIMS, freq, 0.0)
    ang = pos_ref[...].astype(F32) * freq
    cos = jnp.cos(ang)
    sin = jnp.sin(ang)
    c_mul = jnp.where(d < ROT_DIMS, cos, 1.0)
    s_lo = jnp.where(d < half, -sin, 0.0)
    s_hi = jnp.where((d >= half) & (d < ROT_DIMS), sin, 0.0)
    del tm

    def rot(t):
        return t * c_mul + pltpu.roll(t, LANES - half, 1) * s_lo + pltpu.roll(t, half, 1) * s_hi

    scale = DIFF_QK ** -0.5
    for h in range(DIFF_HEADS):
        sl = slice(h * LANES, (h + 1) * LANES)
        qo_ref[:, sl] = (rot(q_ref[:, sl]) * scale).astype(BF16)
        ko_ref[:, sl] = rot(k_ref[:, sl]).astype(BF16)
    vo_ref[...] = v_ref[...].astype(BF16)


def _rope(main, pos):
    T = main.shape[0]
    tm = min(256, T)
    C = DIFF_WIDTH
    base = 3 * RWKV_WIDTH // C
    blk = lambda c: pl.BlockSpec((tm, C), lambda i: (i, base + c))
    out = pl.BlockSpec((tm, C), lambda i: (i, 0))
    return pl.pallas_call(
        _rope_kernel,
        grid=(T // tm,),
        in_specs=[blk(0), blk(1), blk(2), pl.BlockSpec((tm, 1), lambda i: (i, 0))],
        out_specs=[out] * 3,
        out_shape=[jax.ShapeDtypeStruct((T, C), BF16)] * 3,
        compiler_params=_cparams("parallel"),
        name="rope",
    )(main, main, main, pos)


def _flash_kernel(q_ref, k_ref, v_ref, lq1_ref, lk1_ref, lq2_ref, lk2_ref, sub_ref, o_ref, *, lam_init, tq):
    i = pl.program_id(2)
    tk = tq
    q = q_ref[...]
    lane = lax.broadcasted_iota(jnp.int32, q.shape, 1)
    zero = jnp.zeros_like(q)
    qq = jnp.concatenate([jnp.where(lane < DIFF_QK, q, zero), jnp.where(lane >= DIFF_QK, q, zero)], axis=0)

    def block(j, carry, diagonal):
        m_prev, l_prev, acc = carry
        off = pl.multiple_of(j * tk, tk)
        k = k_ref[pl.ds(off, tk), :]
        v = v_ref[pl.ds(off, tk), :]
        s = lax.dot_general(qq, k, _NT, preferred_element_type=F32)
        if diagonal:
            row = lax.broadcasted_iota(jnp.int32, (2 * tq, tk), 0)
            col = lax.broadcasted_iota(jnp.int32, (2 * tq, tk), 1)
            s = jnp.where(col <= jnp.where(row >= tq, row - tq, row), s, -jnp.inf)
        m_new = jnp.maximum(m_prev, jnp.max(s, axis=1, keepdims=True))
        a = jnp.exp(m_prev - m_new)
        p = jnp.exp(s - m_new)
        l_new = a * l_prev + jnp.sum(p, axis=1, keepdims=True)
        acc = a * acc + jnp.dot(p.astype(BF16), v, preferred_element_type=F32)
        return m_new, l_new, acc

    init = (jnp.full((2 * tq, 1), -jnp.inf, F32), jnp.zeros((2 * tq, 1), F32), jnp.zeros((2 * tq, LANES), F32))
    carry = lax.fori_loop(0, i, lambda j, c: block(j, c, False), init)
    _, l_fin, acc = block(i, carry, True)
    lam = (jnp.exp(jnp.sum(lq1_ref[...] * lk1_ref[...], keepdims=True))
           - jnp.exp(jnp.sum(lq2_ref[...] * lk2_ref[...], keepdims=True)) + lam_init)
    on = acc / l_fin
    o = on[:tq] - lam * on[tq:]
    o = o * lax.rsqrt(jnp.mean(o * o, axis=-1, keepdims=True) + LN_EPS) * sub_ref[...]
    o_ref[...] = o * (1.0 - lam_init)


def _diff_attention(q, k, v, B, S, lq1, lk1, lq2, lk2, subw, layer_idx):
    T = q.shape[0]
    tq = min(1024, S)
    nq = S // tq
    lam_init = 0.8 - 0.6 * math.exp(-0.3 * layer_idx)
    vec = pl.BlockSpec((1, DIFF_QK), lambda b, h, i: (0, 0))
    seq = pl.BlockSpec((S, LANES), lambda b, h, i: (b, h))
    return pl.pallas_call(
        functools.partial(_flash_kernel, lam_init=lam_init, tq=tq),
        grid=(B, DIFF_HEADS, nq),
        in_specs=[pl.BlockSpec((tq, LANES), lambda b, h, i: (b * nq + i, h)), seq, seq,
                  vec, vec, vec, vec,
                  pl.BlockSpec((1, DIFF_VHEAD), lambda b, h, i: (0, 0))],
        out_specs=pl.BlockSpec((tq, LANES), lambda b, h, i: (b * nq + i, h)),
        out_shape=jax.ShapeDtypeStruct((T, DIFF_WIDTH), F32),
        compiler_params=_cparams("parallel", "parallel", "parallel"),
        name="diff_attention",
    )(q, k, v, lq1, lk1, lq2, lk2, subw)


def _outproj_kernel(x_ref, ya_ref, g_ref, yb_ref, wa_ref, wb_ref, lnw_ref, lnb_ref, o_ref):
    ya = (ya_ref[...] * g_ref[...]).astype(BF16)
    mix = (jnp.dot(ya, wa_ref[...], preferred_element_type=F32)
           + jnp.dot(yb_ref[...].astype(BF16), wb_ref[...], preferred_element_type=F32))
    o_ref[...] = _layer_norm(DEEPNORM_ALPHA * x_ref[...] + mix, lnw_ref[...], lnb_ref[...])


def _outproj(x, ya, g, yb, wa, wb, lnw, lnb):
    T, D = x.shape
    tm = min(256, T)
    half = pl.BlockSpec((tm, RWKV_WIDTH), lambda i: (i, 0))
    full = pl.BlockSpec((tm, D), lambda i: (i, 0))
    wsp = pl.BlockSpec((RWKV_WIDTH, D), lambda i: (0, 0))
    vec = pl.BlockSpec((1, D), lambda i: (0, 0))
    return pl.pallas_call(
        _outproj_kernel,
        grid=(T // tm,),
        in_specs=[full, half, half, half, wsp, wsp, vec, vec],
        out_specs=full,
        out_shape=jax.ShapeDtypeStruct((T, D), F32),
        compiler_params=_cparams("parallel"),
        name="outproj_ln",
    )(x, ya, g, yb, wa, wb, lnw, lnb)


def _xattn_kernel(x_ref, wq_ref, k_ref, v_ref, wo_ref, lnw_ref, lnb_ref, o_ref):
    x = x_ref[...]
    q = jnp.dot(x.astype(BF16), wq_ref[...], preferred_element_type=F32) * (XATTN_HEAD ** -0.5)
    outs = []
    for h in range(XATTN_HEADS):
        sl = slice(h * XATTN_HEAD, (h + 1) * XATTN_HEAD)
        s = lax.dot_general(q[:, sl].astype(BF16), k_ref[:, sl], _NT, preferred_element_type=F32)
        p = jnp.exp(s - jnp.max(s, axis=-1, keepdims=True))
        p = p / jnp.sum(p, axis=-1, keepdims=True)
        outs.append(jnp.dot(p.astype(BF16), v_ref[:, sl], preferred_element_type=F32))
    o = jnp.concatenate(outs, axis=1).astype(BF16)
    xa = jnp.dot(o, wo_ref[...], preferred_element_type=F32)
    o_ref[...] = _layer_norm(DEEPNORM_ALPHA * x + xa, lnw_ref[...], lnb_ref[...])


def _xattn(x, S, wq, kmem, vmem, wo, lnw, lnb, mem_len):
    T, D = x.shape
    tm = min(256, S)
    tiles_per_seq = S // tm
    full = pl.BlockSpec((tm, D), lambda i: (i, 0))
    kv = pl.BlockSpec((mem_len, XATTN_WIDTH), lambda i: (i // tiles_per_seq, 0))
    vec = pl.BlockSpec((1, D), lambda i: (0, 0))
    return pl.pallas_call(
        _xattn_kernel,
        grid=(T // tm,),
        in_specs=[full, pl.BlockSpec((D, XATTN_WIDTH), lambda i: (0, 0)), kv, kv,
                  pl.BlockSpec((XATTN_WIDTH, D), lambda i: (0, 0)), vec, vec],
        out_specs=full,
        out_shape=jax.ShapeDtypeStruct((T, D), F32),
        compiler_params=_cparams("parallel"),
        name="xattn_ln",
    )(x, wq, kmem, vmem, wo, lnw, lnb)


ROUTE_HEADS_PER_ITER = 4


def _route_kernel(x_ref, pq_ref, sk_ref, e_ref, g_ref, q_s, sv_s, si_s, cand_s, eid_s):
    tm = x_ref.shape[0]
    K = PEER_TOPK
    NK = PEER_KEYS
    q = jnp.dot(x_ref[...].astype(BF16), pq_ref[...], preferred_element_type=F32)
    for hc in range(2 * PEER_HEADS):
        q_s[hc] = q[:, hc * LANES:(hc + 1) * LANES].astype(BF16)
    neg = -jnp.inf
    n_b = [K // (a + 1) for a in range(K)]
    offs = [sum(n_b[:a]) for a in range(K)]
    n_cand = cand_s.shape[1]
    P = ROUTE_HEADS_PER_ITER

    for sub in range(tm // LANES):
        tok = slice(sub * LANES, (sub + 1) * LANES)
        rowk = lax.broadcasted_iota(jnp.int32, (NK, LANES), 0).astype(F32)
        rowc = lax.broadcasted_iota(jnp.int32, (n_cand, LANES), 0).astype(F32)

        def heads(hp, carry):
            for p in range(P):
                s = [lax.dot_general(sk_ref[c], q_s[2 * (hp * P + p) + c, tok, :], _NT,
                                     preferred_element_type=F32) for c in range(2)]
                for j in range(K):
                    for c in range(2):
                        m = jnp.max(s[c], axis=0, keepdims=True)
                        idx = jnp.min(jnp.where(s[c] == m, rowk, float(NK)), axis=0, keepdims=True)
                        s[c] = jnp.where(rowk == idx, neg, s[c])
                        sv_s[p, c, j:j + 1, :] = m
                        si_s[p, c, j:j + 1, :] = idx
                cand_s[p, n_cand - 8:, :] = jnp.full((8, LANES), neg, F32)
                eid_s[p, n_cand - 8:, :] = jnp.zeros((8, LANES), F32)
                for a in range(K):
                    rows = slice(offs[a], offs[a] + n_b[a])
                    cand_s[p, rows, :] = sv_s[p, 0, a:a + 1, :] + sv_s[p, 1, 0:n_b[a], :]
                    eid_s[p, rows, :] = si_s[p, 0, a:a + 1, :] * float(NK) + si_s[p, 1, 0:n_b[a], :]
            cand = [cand_s[p] for p in range(P)]
            eid = [eid_s[p] for p in range(P)]
            fv = [[] for _ in range(P)]
            fe = [[] for _ in range(P)]
            for j in range(K):
                for p in range(P):
                    m = jnp.max(cand[p], axis=0, keepdims=True)
                    idx = jnp.min(jnp.where(cand[p] == m, rowc, float(n_cand)), axis=0, keepdims=True)
                    sel = rowc == idx
                    fe[p].append(jnp.max(jnp.where(sel, eid[p], -1.0), axis=0, keepdims=True))
                    cand[p] = jnp.where(sel, neg, cand[p])
                    fv[p].append(m)
            for p in range(P):
                w = [jnp.exp(fv[p][j] - fv[p][0]) for j in range(K)]
                den = w[0]
                for j in range(1, K):
                    den = den + w[j]
                for j in range(K):
                    sv_s[p, 0, j:j + 1, :] = w[j] / den
                    si_s[p, 0, j:j + 1, :] = fe[p][j]
                base = pl.multiple_of((hp * P + p) * K, K)
                g_ref[pl.ds(base, K), tok] = sv_s[p, 0]
                e_ref[pl.ds(base, K), tok] = si_s[p, 0].astype(jnp.int32)
            return carry

        lax.fori_loop(0, PEER_HEADS // P, heads, 0)


def _route(x, pq, sk):
    T, D = x.shape
    tm = min(256, T)
    out = pl.BlockSpec((PEER_SLOTS, tm), lambda i: (0, i))
    n_cand = sum(PEER_TOPK // (a + 1) for a in range(PEER_TOPK))
    n_cand = -(-n_cand // 8) * 8
    return pl.pallas_call(
        _route_kernel,
        grid=(T // tm,),
        in_specs=[pl.BlockSpec((tm, D), lambda i: (i, 0)),
                  pl.BlockSpec((D, D), lambda i: (0, 0)),
                  pl.BlockSpec((2, PEER_KEYS, LANES), lambda i: (0, 0, 0))],
        out_specs=[out, out],
        out_shape=[jax.ShapeDtypeStruct((PEER_SLOTS, T), jnp.int32),
                   jax.ShapeDtypeStruct((PEER_SLOTS, T), F32)],
        scratch_shapes=[pltpu.VMEM((2 * PEER_HEADS, tm, LANES), BF16),
                        pltpu.VMEM((ROUTE_HEADS_PER_ITER, 2, PEER_TOPK, LANES), F32),
                        pltpu.VMEM((ROUTE_HEADS_PER_ITER, 2, PEER_TOPK, LANES), F32),
                        pltpu.VMEM((ROUTE_HEADS_PER_ITER, n_cand, LANES), F32),
                        pltpu.VMEM((ROUTE_HEADS_PER_ITER, n_cand, LANES), F32)],
        compiler_params=_cparams("parallel"),
        name="peer_route",
    )(x, pq, sk)


EXPERT_TOKENS_PER_STEP = 128
EXPERT_GROUP = 4
EXPERT_SETS = 4
EXPERT_AHEAD = 2


def _expert_kernel(idx_ref, nidx_ref, x_ref, gt_ref, lnw_ref, lnb_ref, uv_ref, o_ref, *scratch):
    G = EXPERT_GROUP
    nbuf = G * EXPERT_SETS
    bufs, sem = scratch[:nbuf], scratch[nbuf]
    TB = x_ref.shape[0]
    NE = PEER_SLOTS
    step = pl.program_id(0)
    per_chunk = NE // (2 * D_CHUNKS)

    def start(iref, t, e, b):
        priority = e % 2 if isinstance(e, int) else 0
        row = iref[t * NE + e]
        pltpu.make_async_copy(uv_ref.at[row], bufs[b].at[:, e, :], sem.at[b]).start(priority=priority)

    def wait(b):
        pltpu.make_async_copy(bufs[b], bufs[b], sem.at[b]).wait()

    @pl.when(step == 0)
    def _():
        for b in range(EXPERT_AHEAD * G):
            def body(e, c, b=b):
                start(idx_ref, b, e, b)
                return c
            lax.fori_loop(0, NE, body, 0)

    lane = lax.broadcasted_iota(jnp.int32, (NE, TB), 1)

    def group(t0, s, issue):
        for k in range(G):
            wait(s * G + k)
        xrow = [x_ref[pl.ds(t0 + k, 1), :] for k in range(G)]

        def u_phase(k):
            buf = bufs[s * G + k]
            acc = jnp.zeros((NE, LANES), F32)
            for c in range(D_CHUNKS):
                for e in range(c * per_chunk, (c + 1) * per_chunk):
                    issue(k, e)
                u = lax.bitcast_convert_type(buf[c] & jnp.uint32(0xFFFF0000), F32)
                acc = acc + u * xrow[k][:, c * LANES:(c + 1) * LANES]
            h = jnp.sum(acc, axis=1, keepdims=True)
            gate = jnp.sum(jnp.where(lane == t0 + k, gt_ref[...], 0.0), axis=1, keepdims=True)
            return 0.5 * h * (1.0 + lax.erf(h * (2.0 ** -0.5))) * gate

        def v_phase(k, a):
            buf = bufs[s * G + k]
            ys = []
            for c in range(D_CHUNKS):
                for e in range((D_CHUNKS + c) * per_chunk, (D_CHUNKS + c + 1) * per_chunk):
                    issue(k, e)
                v = lax.bitcast_convert_type(buf[c] << 16, F32)
                ys.append(jnp.sum(v * a, axis=0, keepdims=True))
            z = DEEPNORM_ALPHA * xrow[k] + jnp.concatenate(ys, axis=1)
            o_ref[t0 + k] = _layer_norm(z, lnw_ref[...], lnb_ref[...])

        a = u_phase(0)
        for k in range(G):
            a_next = u_phase(k + 1) if k + 1 < G else None
            v_phase(k, a)
            a = a_next

    def ring(t0, last):
        for s in range(EXPERT_SETS):
            ns = (s + EXPERT_AHEAD) % EXPERT_SETS
            if last and s + EXPERT_AHEAD >= EXPERT_SETS:
                issue = lambda k, e, g0=(s + EXPERT_AHEAD - EXPERT_SETS) * G, ns=ns: start(
                    nidx_ref, g0 + k, e, ns * G + k)
            else:
                issue = lambda k, e, g0=t0 + (s + EXPERT_AHEAD) * G, ns=ns: start(idx_ref, g0 + k, e, ns * G + k)
            group(t0 + s * G, s, issue)

    def ring_body(q, c):
        ring(q * (G * EXPERT_SETS), False)
        return c

    lax.fori_loop(0, TB // (G * EXPERT_SETS) - 1, ring_body, 0)
    ring(TB - G * EXPERT_SETS, True)

    @pl.when(step == pl.num_programs(0) - 1)
    def _():
        for b in range(EXPERT_AHEAD * G):
            wait(b)


def _experts(exp_flat, x, gates_t, lnw, lnb, uv):
    T, D = x.shape
    TB = min(EXPERT_TOKENS_PER_STEP, T)
    NE = PEER_SLOTS
    n = T // TB
    nbuf = EXPERT_GROUP * EXPERT_SETS
    out = pl.pallas_call(
        _expert_kernel,
        grid=(n,),
        in_specs=[pl.BlockSpec((TB * NE,), lambda i: (i,), memory_space=pltpu.SMEM),
                  pl.BlockSpec((TB * NE,), lambda i: (jnp.minimum(i + 1, n - 1),), memory_space=pltpu.SMEM),
                  pl.BlockSpec((TB, D), lambda i: (i, 0)),
                  pl.BlockSpec((NE, TB), lambda i: (0, i)),
                  pl.BlockSpec((1, D), lambda i: (0, 0)),
                  pl.BlockSpec((1, D), lambda i: (0, 0)),
                  pl.BlockSpec(memory_space=pl.ANY)],
        out_specs=pl.BlockSpec((TB, 1, D), lambda i: (i, 0, 0)),
        out_shape=jax.ShapeDtypeStruct((T, 1, D), F32),
        scratch_shapes=[pltpu.VMEM((D_CHUNKS, NE, LANES), jnp.uint32)] * nbuf
        + [pltpu.SemaphoreType.DMA((nbuf,))],
        compiler_params=_cparams("arbitrary"),
        name="peer_experts",
    )(exp_flat, exp_flat, x, gates_t, lnw, lnb, uv)
    return out.reshape(T, D)


def kernel(x, mem, positions, w_in, shift_mu, w0, w_up, a0, a_up, g_up, k_k, k_a, r_k, lnx_w, lnx_b, lam_q1, lam_k1, lam_q2, lam_k2, subln_w, w_out, ln1_w, ln1_b, xq, xk, xv, xo, ln2_w, ln2_b, pq, subkeys, peer_u, peer_v, ln3_w, ln3_b):
    B, S, D = x.shape
    T = B * S
    M = mem.shape[1]
    H, N, C = RWKV_HEADS, RWKV_HEAD, RWKV_WIDTH
    xt = x.reshape(T, D)
    memt = mem.reshape(B * M, D)
    pos = positions.reshape(T, 1)
    row = lambda a: a.reshape(1, -1)
    lora_pad = LORA_PAD - LORA_COLS

    for l in range(DEPTH):
        wl = w_in[l]
        w_main = jnp.concatenate([wl[:, :3 * C], wl[:, RWKV_COLS:]], axis=1).astype(BF16)
        w_lora = jnp.pad(wl[:, 3 * C:RWKV_COLS], ((0, 0), (0, lora_pad))).astype(BF16)
        mu = shift_mu[l]
        mu_l = jnp.pad(mu[3 * C:], (0, lora_pad))
        o1, o2 = DECAY_LORA, DECAY_LORA + AAA_LORA
        wup = jnp.pad(w_up[l], ((0, LORA_PAD - o1), (0, 0))).astype(BF16)
        aup = jnp.pad(a_up[l], ((o1, LORA_PAD - o2), (0, 0))).astype(BF16)
        gup = jnp.pad(g_up[l], ((o2, lora_pad), (0, 0))).astype(BF16)

        main = _matmul(xt, w_main, F32, 512, 1024)
        lora = _matmul(xt, w_lora, F32, 512, LORA_PAD)
        r, lw, k2, v, kkr, ic, g = _rwkv_prep(
            main, lora, S, row(mu[:C]), row(mu[C:2 * C]), row(mu[2 * C:3 * C]), row(mu_l),
            row(w0[l]), row(a0[l]), row(k_k[l]), row(k_a[l]), wup, aup, gup)
        ya = _rwkv_scan(r, lw, k2, v, kkr, ic,
                        r_k[l].reshape(H, 1, N), lnx_w[l].reshape(H, 1, N), lnx_b[l].reshape(H, 1, N), S)

        qr, kr, vb = _rope(main, pos)
        yb = _diff_attention(qr, kr, vb, B, S, row(lam_q1[l]), row(lam_k1[l]), row(lam_q2[l]), row(lam_k2[l]),
                             row(subln_w[l]), l)

        wo = w_out[l].astype(BF16)
        x1 = _outproj(xt, ya, g, yb, wo[:C], wo[C:], row(ln1_w[l]), row(ln1_b[l]))

        kmem = _matmul(memt, xk[l].astype(BF16), BF16, 512, XATTN_WIDTH)
        vmem = _matmul(memt, xv[l].astype(BF16), BF16, 512, XATTN_WIDTH)
        x2 = _xattn(x1, S, xq[l].astype(BF16), kmem, vmem, xo[l].astype(BF16), row(ln2_w[l]), row(ln2_b[l]), M)

        exp_t, gates_t = _route(x2, pq[l].astype(BF16), subkeys[l].astype(BF16))
        half = lambda w: lax.bitcast_convert_type(w.astype(BF16), jnp.uint16).astype(jnp.uint32)
        uv = ((half(peer_u[l]) << 16) | half(peer_v[l])).reshape(PEER_EXPERTS, D_CHUNKS, LANES)
        xt = _experts(exp_t.T.reshape(-1), x2, gates_t, row(ln3_w[l]), row(ln3_b[l]), uv)
    return xt.reshape(B, S, D)
```
